```python
import jax
import jax.numpy as jnp
from jax import lax
import numpy as np


D_MODEL = 1024
BATCH = 8
SEQ = 4096
DEPTH = 1

CHUNK = 64
MEM_LEN = 256
HEAD_DIM = 64
CONV_CH = D_MODEL // 4
CONV_WIDTH = 31
SB_HEADS = D_MODEL // (2 * HEAD_DIM)
MEM_HEADS = 4
SB_WIDTH = SB_HEADS * HEAD_DIM
MEM_WIDTH = MEM_HEADS * HEAD_DIM
MIX_WIDTH = CONV_CH + SB_WIDTH + MEM_WIDTH
IN_WIDTH = 2 * CONV_CH + 3 * SB_WIDTH + MEM_WIDTH
Q_BLOCK = 128
PEER_HEADS = 8
N_KEYS = 128
N_EXPERTS = N_KEYS * N_KEYS
PEER_QUERY_DIM = 256
PEER_HALF = PEER_QUERY_DIM // 2
PEER_TOPK = 16
PEER_TOKEN_BLOCK = 128
EPS = 1e-6

kernel_name = 'hybrid_conv_stickbreak_peer_block'


def rms_norm(x, g):
    x32 = x.astype(jnp.float32)
    y = x32 * lax.rsqrt(jnp.mean(x32 * x32, axis=-1, keepdims=True) + EPS)
    return (y * g.astype(jnp.float32)).astype(x.dtype)


def layer_norm(x, g, b):
    x32 = x.astype(jnp.float32)
    mu = jnp.mean(x32, axis=-1, keepdims=True)
    var = jnp.mean(jnp.square(x32 - mu), axis=-1, keepdims=True)
    y = (x32 - mu) * lax.rsqrt(var + EPS) * g.astype(jnp.float32) + b.astype(jnp.float32)
    return y.astype(x.dtype)


def conformer_conv(a, gate, conv_w, conv_b, ln_g, ln_b):
    u = a * jax.nn.sigmoid(gate)
    rhs = conv_w[:, None, :].astype(u.dtype)
    y = lax.conv_general_dilated(u, rhs, window_strides=(1,), padding=[(CONV_WIDTH - 1, 0)],
                                 dimension_numbers=('NWC', 'WIO', 'NWC'),
                                 feature_group_count=CONV_CH)
    y = y + conv_b.astype(y.dtype)
    return jax.nn.silu(layer_norm(y, ln_g, ln_b))


def stick_breaking_attention(q, k, v):
    seq = q.shape[1]
    scale = HEAD_DIM ** -0.5
    qh = jnp.swapaxes(q, 1, 2).astype(jnp.float32)
    kh = jnp.swapaxes(k, 1, 2).astype(jnp.float32)
    vh = jnp.swapaxes(v, 1, 2).astype(jnp.float32)
    outs = []
    for i in range(seq // Q_BLOCK):
        q0 = i * Q_BLOCK
        kl = q0 + Q_BLOCK
        z = jnp.einsum('bhqd,bhkd->bhqk', qh[:, :, q0:kl], kh[:, :, :kl]) * scale
        t_pos = q0 + jnp.arange(Q_BLOCK)[:, None]
        s_pos = jnp.arange(kl)[None, :]
        mask = s_pos < t_pos
        log_1m = jnp.where(mask, jax.nn.log_sigmoid(-z), 0.0)
        after = lax.cumsum(log_1m, axis=3, reverse=True) - log_1m
        a = jnp.where(mask, jnp.exp(jax.nn.log_sigmoid(z) + after), 0.0)
        outs.append(jnp.einsum('bhqk,bhkd->bhqd', a, vh[:, :, :kl]))
    out = jnp.concatenate(outs, axis=2)
    return jnp.swapaxes(out, 1, 2).astype(q.dtype)


def memory_cross_attention(q, mem_n, w_kv, q_g, k_g):
    b, m, _ = mem_n.shape
    kv = (mem_n @ w_kv).reshape(b, m, 2, MEM_HEADS, HEAD_DIM)
    k = rms_norm(kv[:, :, 0], k_g)
    v = kv[:, :, 1]
    qn = rms_norm(q, q_g)
    scores = jnp.einsum('bshd,bmhd->bhsm', qn.astype(jnp.float32), k.astype(jnp.float32)) * (HEAD_DIM ** -0.5)
    p = jax.nn.softmax(scores, axis=-1)
    o = jnp.einsum('bhsm,bmhd->bshd', p, v.astype(jnp.float32))
    return o.astype(q.dtype)


def peer_ffn(h, w_q, keys1, keys2, u, v):
    b, n, d = h.shape
    blocks = h.reshape(-1, PEER_TOKEN_BLOCK, d)

    def one_block(hb):
        q = (hb @ w_q).reshape(PEER_TOKEN_BLOCK, PEER_HEADS, 2, PEER_HALF)
        s1 = jnp.einsum('thc,hnc->thn', q[:, :, 0], keys1)
        s2 = jnp.einsum('thc,hnc->thn', q[:, :, 1], keys2)
        v1, i1 = lax.top_k(s1, PEER_TOPK)
        v2, i2 = lax.top_k(s2, PEER_TOPK)
        n_cand = PEER_TOPK * PEER_TOPK
        cand_s = (v1[..., :, None] + v2[..., None, :]).reshape(PEER_TOKEN_BLOCK, PEER_HEADS, n_cand)
        cand_i = (i1[..., :, None] * N_KEYS + i2[..., None, :]).reshape(PEER_TOKEN_BLOCK, PEER_HEADS, n_cand)
        top_s, pos = lax.top_k(cand_s, PEER_TOPK)
        eidx = jnp.take_along_axis(cand_i, pos, axis=-1)
        gates = jax.nn.softmax(top_s.astype(jnp.float32), axis=-1).astype(hb.dtype)
        u_sel = u[eidx]
        act = jax.nn.gelu(jnp.einsum('thkd,td->thk', u_sel, hb), approximate=False)
        return jnp.einsum('thk,thkd->td', gates * act, v[eidx])

    out = lax.map(one_block, blocks)
    return out.reshape(b, n, d)


def setup_inputs(seed: int = 0) -> dict:
    key = jax.random.key(seed)
    ks = jax.random.split(key, 24)

    def nrm(k, shape, scale):
        return jax.random.normal(k, shape, jnp.float32) * scale

    def gain(k, shape):
        return 1.0 + 0.01 * jax.random.normal(k, shape, jnp.float32)

    L = DEPTH
    return {
        'x': nrm(ks[0], (BATCH, SEQ, D_MODEL), 1.0),
        'mem': nrm(ks[1], (BATCH, MEM_LEN, D_MODEL), 1.0),
        'attn_norm_g': gain(ks[2], (L, D_MODEL)),
        'w_in': nrm(ks[3], (L, D_MODEL, IN_WIDTH), D_MODEL ** -0.5),
        'conv_w': nrm(ks[4], (L, CONV_WIDTH, CONV_CH), CONV_WIDTH ** -0.5),
        'conv_b': nrm(ks[5], (L, CONV_CH), 0.01),
        'conv_ln_g': gain(ks[6], (L, CONV_CH)),
        'conv_ln_b': nrm(ks[7], (L, CONV_CH), 0.01),
        'sb_out_g': gain(ks[8], (L, SB_WIDTH)),
        'mem_norm_g': gain(ks[9], (L, D_MODEL)),
        'w_mem_kv': nrm(ks[10], (L, D_MODEL, 2 * MEM_WIDTH), D_MODEL ** -0.5),
        'q_norm_g': gain(ks[11], (L, HEAD_DIM)),
        'k_norm_g': gain(ks[12], (L, HEAD_DIM)),
        'mem_out_g': gain(ks[13], (L, MEM_WIDTH)),
        'w_out': nrm(ks[14], (L, MIX_WIDTH, D_MODEL), MIX_WIDTH ** -0.5),
        'ffn_norm_g': gain(ks[15], (L, D_MODEL)),
        'w_peer_q': nrm(ks[16], (L, D_MODEL, PEER_HEADS * PEER_QUERY_DIM), D_MODEL ** -0.5),
        'peer_keys1': nrm(ks[17], (L, PEER_HEADS, N_KEYS, PEER_HALF), PEER_HALF ** -0.5),
        'peer_keys2': nrm(ks[18], (L, PEER_HEADS, N_KEYS, PEER_HALF), PEER_HALF ** -0.5),
        'peer_u': nrm(ks[19], (L, N_EXPERTS, D_MODEL), D_MODEL ** -0.5),
        'peer_v': nrm(ks[20], (L, N_EXPERTS, D_MODEL), 0.5),
    }


def reference(x, mem, attn_norm_g, w_in, conv_w, conv_b, conv_ln_g, conv_ln_b, sb_out_g,
              mem_norm_g, w_mem_kv, q_norm_g, k_norm_g, mem_out_g, w_out, ffn_norm_g,
              w_peer_q, peer_keys1, peer_keys2, peer_u, peer_v):
    b, n, _ = x.shape
    splits = [CONV_CH, 2 * CONV_CH, 2 * CONV_CH + SB_WIDTH, 2 * CONV_CH + 2 * SB_WIDTH,
              2 * CONV_CH + 3 * SB_WIDTH]
    for l in range(DEPTH):
        h = rms_norm(x, attn_norm_g[l])
        proj = h @ w_in[l]
        c_a, c_gate, sb_q, sb_k, sb_v, m_q = jnp.split(proj, splits, axis=-1)
        conv_o = conformer_conv(c_a, c_gate, conv_w[l], conv_b[l], conv_ln_g[l], conv_ln_b[l])
        sb = stick_breaking_attention(sb_q.reshape(b, n, SB_HEADS, HEAD_DIM),
                                      sb_k.reshape(b, n, SB_HEADS, HEAD_DIM),
                                      sb_v.reshape(b, n, SB_HEADS, HEAD_DIM))
        sb_o = rms_norm(sb.reshape(b, n, SB_WIDTH), sb_out_g[l])
        mem_n = rms_norm(mem, mem_norm_g[l])
        mo = memory_cross_attention(m_q.reshape(b, n, MEM_HEADS, HEAD_DIM), mem_n, w_mem_kv[l],
                                    q_norm_g[l], k_norm_g[l])
        mem_o = rms_norm(mo.reshape(b, n, MEM_WIDTH), mem_out_g[l])
        mixed = jnp.concatenate([conv_o, sb_o, mem_o], axis=-1) @ w_out[l]
        x = x + mixed.astype(x.dtype)
        h2 = rms_norm(x, ffn_norm_g[l])
        x = x + peer_ffn(h2, w_peer_q[l], peer_keys1[l], peer_keys2[l], peer_u[l], peer_v[l]).astype(x.dtype)
    return x
```

```python
import functools

import jax
import jax.numpy as jnp
from jax import lax
from jax.experimental import pallas as pl
from jax.experimental.pallas import tpu as pltpu

F32 = jnp.float32
BF16 = jnp.bfloat16
EPS = 1e-6

HEAD_DIM = 64
LANES = 128
CONV_HALO = 32
PEER_TOPK = 16
N_KEYS = 128
NEG_INF = float("-inf")
BIG_I32 = 2 ** 30
VMEM_LIMIT = 56 * 1024 * 1024


def _rms(x, g):
    return x * lax.rsqrt(jnp.mean(x * x, axis=-1, keepdims=True) + EPS) * g


def _dot_nt(a, b):
    return lax.dot_general(a, b, (((1,), (1,)), ((), ())), preferred_element_type=F32)


def _split_bf16(x):
    hi = x.astype(BF16)
    lo = (x - hi.astype(F32)).astype(BF16)
    return hi, lo


def _in_proj_kernel(x_ref, g_ref, w_ref, o_ref):
    h = _rms(x_ref[...], g_ref[...])
    o_ref[...] = jnp.dot(h.astype(BF16), w_ref[...], preferred_element_type=F32).astype(o_ref.dtype)


def _in_proj(x2, g, w, tm):
    n, d = x2.shape
    width = w.shape[1]
    return pl.pallas_call(
        _in_proj_kernel,
        grid=(n // tm,),
        in_specs=[pl.BlockSpec((tm, d), lambda i: (i, 0)),
                  pl.BlockSpec((1, d), lambda i: (0, 0)),
                  pl.BlockSpec((d, width), lambda i: (0, 0))],
        out_specs=pl.BlockSpec((tm, width), lambda i: (i, 0)),
        out_shape=jax.ShapeDtypeStruct((n, width), BF16),
        compiler_params=pltpu.CompilerParams(dimension_semantics=("parallel",),
                                             vmem_limit_bytes=VMEM_LIMIT),
        name="in_proj",
    )(x2, g, w)


def _conv_kernel(a_ref, gate_ref, w_ref, b_ref, lg_ref, lb_ref, o_ref, ubuf, *, tc, width):
    i = pl.program_id(1)

    @pl.when(i == 0)
    def _():
        ubuf[0:CONV_HALO, :] = jnp.zeros((CONV_HALO, ubuf.shape[1]), F32)

    @pl.when(i > 0)
    def _():
        ubuf[0:CONV_HALO, :] = ubuf[tc:tc + CONV_HALO, :]

    a = a_ref[...].astype(F32)
    gate = gate_ref[...].astype(F32)
    ubuf[CONV_HALO:CONV_HALO + tc, :] = a * jax.nn.sigmoid(gate)
    base = CONV_HALO - (width - 1)
    acc = jnp.zeros((tc, ubuf.shape[1]), F32)
    for j in range(width):
        acc = acc + w_ref[j:j + 1, :] * ubuf[base + j:base + j + tc, :]
    y = acc + b_ref[...]
    mu = jnp.mean(y, axis=-1, keepdims=True)
    yc = y - mu
    var = jnp.mean(yc * yc, axis=-1, keepdims=True)
    z = yc * lax.rsqrt(var + EPS) * lg_ref[...] + lb_ref[...]
    o_ref[...] = (z * jax.nn.sigmoid(z)).astype(o_ref.dtype)


def _conv(proj, conv_w, conv_b, ln_g, ln_b, batch, seq, tc):
    n = proj.shape[0]
    width, ch = conv_w.shape
    nt = seq // tc
    kern = functools.partial(_conv_kernel, tc=tc, width=width)
    vec = lambda: pl.BlockSpec((1, ch), lambda b, i: (0, 0))
    return pl.pallas_call(
        kern,
        grid=(batch, nt),
        in_specs=[pl.BlockSpec((tc, ch), lambda b, i: (b * nt + i, 0)),
                  pl.BlockSpec((tc, ch), lambda b, i: (b * nt + i, 1)),
                  pl.BlockSpec((width, ch), lambda b, i: (0, 0)),
                  vec(), vec(), vec()],
        out_specs=pl.BlockSpec((tc, ch), lambda b, i: (b * nt + i, 0)),
        out_shape=jax.ShapeDtypeStruct((n, ch), BF16),
        scratch_shapes=[pltpu.VMEM((tc + CONV_HALO, ch), F32)],
        compiler_params=pltpu.CompilerParams(dimension_semantics=("parallel", "arbitrary"),
                                             vmem_limit_bytes=VMEM_LIMIT),
        name="conv",
    )(proj, proj, conv_w, conv_b, ln_g, ln_b)


def _sb_kernel(q_ref, k_ref, v_ref, o_ref, acc_ref, carry_ref, *, tq):
    qi = pl.program_id(2)
    q2 = q_ref[...]
    lane = lax.broadcasted_iota(jnp.int32, (1, LANES), 1)
    row = lax.broadcasted_iota(jnp.int32, (tq, tq), 0)
    col = lax.broadcasted_iota(jnp.int32, (tq, tq), 1)
    causal = col < row
    suffix = jnp.where(row > col, -1.0, 0.0).astype(BF16)
    scale = HEAD_DIM ** -0.5

    def block(kb, h, qh, diag):
        start = pl.multiple_of(kb * tq, tq)
        kblk = k_ref[pl.ds(start, tq), :]
        vblk = v_ref[pl.ds(start, tq), :]
        z = _dot_nt(qh, kblk)
        sp = jnp.maximum(z, 0.0) + jnp.log(1.0 + jnp.exp(-jnp.abs(z)))
        if diag:
            sp = jnp.where(causal, sp, 0.0)
        after = jnp.dot(sp.astype(BF16), suffix, preferred_element_type=F32) + carry_ref[h]
        a = jnp.exp((z - sp) + after)
        if diag:
            a = jnp.where(causal, a, 0.0)
        acc_ref[h] += jnp.dot(a.astype(BF16), vblk, preferred_element_type=F32)
        carry_ref[h] -= jnp.sum(sp, axis=1, keepdims=True)

    for h in range(2):
        head = (lane >= h * HEAD_DIM) & (lane < (h + 1) * HEAD_DIM)
        qh = jnp.where(head, q2, jnp.zeros_like(q2)) * jnp.asarray(scale, q2.dtype)
        acc_ref[h] = jnp.zeros(acc_ref.shape[1:], F32)
        carry_ref[h] = jnp.zeros(carry_ref.shape[1:], F32)
        block(qi, h, qh, True)

        def body(i, c, h=h, qh=qh):
            block(qi - 1 - i, h, qh, False)
            return c

        lax.fori_loop(0, qi, body, 0)
    o_ref[...] = jnp.where(lane < HEAD_DIM, acc_ref[0], acc_ref[1]).astype(o_ref.dtype)


def _stickbreak(proj, batch, seq, tq, q_col, k_col, v_col, n_pairs):
    n = proj.shape[0]
    nq = seq // tq
    kern = functools.partial(_sb_kernel, tq=tq)
    return pl.pallas_call(
        kern,
        grid=(batch, n_pairs, nq),
        in_specs=[pl.BlockSpec((tq, LANES), lambda b, p, i: (b * nq + i, q_col + p)),
                  pl.BlockSpec((seq, LANES), lambda b, p, i: (b, k_col + p)),
                  pl.BlockSpec((seq, LANES), lambda b, p, i: (b, v_col + p))],
        out_specs=pl.BlockSpec((tq, LANES), lambda b, p, i: (b * nq + i, p)),
        out_shape=jax.ShapeDtypeStruct((n, n_pairs * LANES), F32),
        scratch_shapes=[pltpu.VMEM((2, tq, LANES), F32), pltpu.VMEM((2, tq, 1), F32)],
        compiler_params=pltpu.CompilerParams(
            dimension_semantics=("parallel", "parallel", "arbitrary"),
            vmem_limit_bytes=VMEM_LIMIT),
        name="stickbreak",
    )(proj, proj, proj)


def _head_sumsq(x, blk_ones):
    hi, lo = _split_bf16(x * x)
    return (jnp.dot(hi, blk_ones, preferred_element_type=F32)
            + jnp.dot(lo, blk_ones, preferred_element_type=F32))


def _block_ones(width):
    r = lax.broadcasted_iota(jnp.int32, (width, width), 0) // HEAD_DIM
    c = lax.broadcasted_iota(jnp.int32, (width, width), 1) // HEAD_DIM
    return jnp.where(r == c, 1.0, 0.0).astype(BF16)


def _mem_kv_kernel(mem_ref, g_ref, w_ref, kg_ref, k_ref, v_ref, *, width):
    mn = _rms(mem_ref[...], g_ref[...])
    kv = jnp.dot(mn.astype(BF16), w_ref[...], preferred_element_type=F32)
    k = kv[:, :width]
    ss = _head_sumsq(k, _block_ones(width))
    k_ref[...] = (k * lax.rsqrt(ss * (1.0 / HEAD_DIM) + EPS) * kg_ref[...]).astype(k_ref.dtype)
    v_ref[...] = kv[:, width:].astype(v_ref.dtype)


def _mem_kv(mem, g, w, kg_t):
    b, m, d = mem.shape
    width = w.shape[1] // 2
    kern = functools.partial(_mem_kv_kernel, width=width)
    out = jax.ShapeDtypeStruct((b, m, width), BF16)
    return pl.pallas_call(
        kern,
        grid=(b,),
        in_specs=[pl.BlockSpec((None, m, d), lambda i: (i, 0, 0)),
                  pl.BlockSpec((1, d), lambda i: (0, 0)),
                  pl.BlockSpec((d, 2 * width), lambda i: (0, 0)),
                  pl.BlockSpec((1, width), lambda i: (0, 0))],
        out_specs=[pl.BlockSpec((None, m, width), lambda i: (i, 0, 0)),
                   pl.BlockSpec((None, m, width), lambda i: (i, 0, 0))],
        out_shape=[out, out],
        compiler_params=pltpu.CompilerParams(dimension_semantics=("parallel",),
                                             vmem_limit_bytes=VMEM_LIMIT),
        name="mem_kv",
    )(mem, g, w, kg_t)


def _mem_attn_kernel(q_ref, k_ref, v_ref, qg_ref, og_ref, o_ref, *, width):
    q = q_ref[...].astype(F32)
    ss = _head_sumsq(q, _block_ones(width))
    qn = q * lax.rsqrt(ss * (1.0 / HEAD_DIM) + EPS) * qg_ref[...]
    k = k_ref[...]
    v = v_ref[...]
    lane = lax.broadcasted_iota(jnp.int32, (1, width), 1)
    out = jnp.zeros(q.shape, F32)
    for h in range(width // HEAD_DIM):
        head = (lane >= h * HEAD_DIM) & (lane < (h + 1) * HEAD_DIM)
        qh = jnp.where(head, qn, 0.0).astype(BF16)
        s = _dot_nt(qh, k) * (HEAD_DIM ** -0.5)
        p = jnp.exp(s - jnp.max(s, axis=-1, keepdims=True))
        o = jnp.dot(p.astype(BF16), v, preferred_element_type=F32) / jnp.sum(p, axis=-1, keepdims=True)
        out = jnp.where(head, o, out)
    o_ref[...] = _rms(out, og_ref[...]).astype(o_ref.dtype)


def _mem_attn(proj, k, v, qg_t, og, batch, seq, tm, q_col):
    n = proj.shape[0]
    _, m, width = k.shape
    nt = seq // tm
    kern = functools.partial(_mem_attn_kernel, width=width)
    return pl.pallas_call(
        kern,
        grid=(batch, nt),
        in_specs=[pl.BlockSpec((tm, width), lambda b, i: (b * nt + i, q_col)),
                  pl.BlockSpec((None, m, width), lambda b, i: (b, 0, 0)),
                  pl.BlockSpec((None, m, width), lambda b, i: (b, 0, 0)),
                  pl.BlockSpec((1, width), lambda b, i: (0, 0)),
                  pl.BlockSpec((1, width), lambda b, i: (0, 0))],
        out_specs=pl.BlockSpec((tm, width), lambda b, i: (b * nt + i, 0)),
        out_shape=jax.ShapeDtypeStruct((n, width), BF16),
        compiler_params=pltpu.CompilerParams(dimension_semantics=("parallel", "parallel"),
                                             vmem_limit_bytes=VMEM_LIMIT),
        name="mem_attn",
    )(proj, k, v, qg_t, og)


def _out_proj_kernel(conv_ref, sb_ref, mem_ref, x_ref, sbg_ref, wo_ref, fg_ref, wq_ref,
                     x1_ref, h2_ref, q_ref, *, c_w, s_w):
    sbn = _rms(sb_ref[...], sbg_ref[...]).astype(BF16)
    mixed = jnp.dot(conv_ref[...], wo_ref[0:c_w, :], preferred_element_type=F32)
    mixed = mixed + jnp.dot(sbn, wo_ref[c_w:c_w + s_w, :], preferred_element_type=F32)
    mixed = mixed + jnp.dot(mem_ref[...], wo_ref[c_w + s_w:, :], preferred_element_type=F32)
    x1 = x_ref[...] + mixed
    h2 = _rms(x1, fg_ref[...])
    x1_ref[...] = x1
    h2_ref[...] = h2
    q_ref[...] = jnp.dot(h2.astype(BF16), wq_ref[...], preferred_element_type=F32).astype(q_ref.dtype)


def _out_proj(conv_o, sb, mem_o, x2, sbg, w_out, fg, w_q, tm):
    n, d = x2.shape
    c_w, s_w, m_w = conv_o.shape[1], sb.shape[1], mem_o.shape[1]
    qw = w_q.shape[1]
    kern = functools.partial(_out_proj_kernel, c_w=c_w, s_w=s_w)
    row = lambda w: pl.BlockSpec((tm, w), lambda i: (i, 0))
    full = lambda r, c: pl.BlockSpec((r, c), lambda i: (0, 0))
    return pl.pallas_call(
        kern,
        grid=(n // tm,),
        in_specs=[row(c_w), row(s_w), row(m_w), row(d), full(1, s_w), full(c_w + s_w + m_w, d),
                  full(1, d), full(d, qw)],
        out_specs=[row(d), row(d), row(qw)],
        out_shape=[jax.ShapeDtypeStruct((n, d), F32), jax.ShapeDtypeStruct((n, d), F32),
                   jax.ShapeDtypeStruct((n, qw), BF16)],
        compiler_params=pltpu.CompilerParams(dimension_semantics=("parallel",),
                                             vmem_limit_bytes=VMEM_LIMIT),
        name="out_proj",
    )(conv_o, sb, mem_o, x2, sbg, w_out, fg, w_q)


def _topk_rows(s, ids, k):
    vals, sel = [], []
    for _ in range(k):
        m = jnp.max(s, axis=0, keepdims=True)
        i = jnp.min(jnp.where(s == m, ids, BIG_I32), axis=0, keepdims=True)
        s = jnp.where(ids == i, NEG_INF, s)
        vals.append(m)
        sel.append(i)
    return jnp.concatenate(vals, axis=0), jnp.concatenate(sel, axis=0)


def _peer_topk_kernel(q_ref, k1_ref, k2_ref, eidx_ref, gates_ref, *, half):
    t = q_ref.shape[0]
    key_ids = lax.broadcasted_iota(jnp.int32, (N_KEYS, t), 0)
    q = q_ref[...]
    v1, i1 = _topk_rows(_dot_nt(k1_ref[...], q[:, :half]), key_ids, PEER_TOPK)
    v2, i2 = _topk_rows(_dot_nt(k2_ref[...], q[:, half:]), key_ids, PEER_TOPK)
    cand_s, cand_i = [v1[0:1] + v2], [i1[0:1] * N_KEYS + i2]
    sub = lax.broadcasted_iota(jnp.int32, (8, t), 0)
    for a in range(1, 8):
        ok = sub < (PEER_TOPK // (a + 1))
        cand_s.append(jnp.where(ok, v1[a:a + 1] + v2[0:8], NEG_INF))
        cand_i.append(i1[a:a + 1] * N_KEYS + i2[0:8])
    cand_s.append(v1[8:16] + v2[0:1])
    cand_i.append(i1[8:16] * N_KEYS + i2[0:1])
    top_s, eidx = _topk_rows(jnp.concatenate(cand_s, axis=0), jnp.concatenate(cand_i, axis=0), PEER_TOPK)
    p = jnp.exp(top_s - top_s[0:1])
    eidx_ref[...] = eidx
    gates_ref[...] = p / jnp.sum(p, axis=0, keepdims=True)


def _peer_topk(q, k1, k2, tt):
    n, qw = q.shape
    heads, n_keys, half = k1.shape
    kern = functools.partial(_peer_topk_kernel, half=half)
    out_spec = lambda: pl.BlockSpec((PEER_TOPK, tt), lambda i, h: (h, i))
    return pl.pallas_call(
        kern,
        grid=(n // tt, heads),
        in_specs=[pl.BlockSpec((tt, 2 * half), lambda i, h: (i, h)),
                  pl.BlockSpec((None, n_keys, half), lambda i, h: (h, 0, 0)),
                  pl.BlockSpec((None, n_keys, half), lambda i, h: (h, 0, 0))],
        out_specs=[out_spec(), out_spec()],
        out_shape=[jax.ShapeDtypeStruct((heads * PEER_TOPK, n), jnp.int32),
                   jax.ShapeDtypeStruct((heads * PEER_TOPK, n), F32)],
        compiler_params=pltpu.CompilerParams(dimension_semantics=("parallel", "parallel"),
                                             vmem_limit_bytes=VMEM_LIMIT),
        name="peer_topk",
    )(q, k1, k2)


def _unpack_pair(w):
    lo = pltpu.bitcast(w << 16, F32)
    hi = pltpu.bitcast(w & jnp.uint32(0xFFFF0000), F32)
    return lo, hi


def _peer_u_kernel(idx_ref, tbl_ref, h2_ref, gates_ref, coef_ref, y_ref, act_ref, *, tg, picks):
    ones8 = jnp.ones((8, LANES), BF16)

    def token(t, c):
        hv = h2_ref[t]
        h_lo, h_hi = hv[0:4], hv[4:8]
        for p in range(picks):
            lo, hi = _unpack_pair(tbl_ref[idx_ref[p, t]])
            y_ref[4 * p:4 * p + 4, :] = lo * h_lo + hi * h_hi
        part = (y_ref[pl.ds(0, picks, stride=4), :] + y_ref[pl.ds(1, picks, stride=4), :]
                + y_ref[pl.ds(2, picks, stride=4), :] + y_ref[pl.ds(3, picks, stride=4), :])
        hi, lo = _split_bf16(part)
        act_ref[pl.ds(t, 1), :] = (_dot_nt(ones8, hi) + _dot_nt(ones8, lo))[0:1]
        return c

    lax.fori_loop(0, tg, token, 0)
    act = act_ref[...]
    gelu = 0.5 * act * (1.0 + lax.erf(act * (2.0 ** -0.5)))
    coef_ref[...] = gates_ref[...].T * gelu


def _peer_u(eidx, tbl, h2r, gates, tg):
    picks, n = eidx.shape
    kern = functools.partial(_peer_u_kernel, tg=tg, picks=picks)
    return pl.pallas_call(
        kern,
        grid=(n // tg,),
        in_specs=[pl.BlockSpec((picks, tg), lambda i: (0, i), memory_space=pltpu.SMEM),
                  pl.BlockSpec(memory_space=pltpu.VMEM),
                  pl.BlockSpec((tg, 8, LANES), lambda i: (i, 0, 0)),
                  pl.BlockSpec((picks, tg), lambda i: (0, i))],
        out_specs=pl.BlockSpec((tg, picks), lambda i: (i, 0)),
        out_shape=jax.ShapeDtypeStruct((n, picks), F32),
        scratch_shapes=[pltpu.VMEM((4 * picks, LANES), F32), pltpu.VMEM((tg, picks), F32)],
        compiler_params=pltpu.CompilerParams(dimension_semantics=("parallel",),
                                             vmem_limit_bytes=VMEM_LIMIT),
        name="peer_u",
    )(eidx, tbl, h2r, gates)


def _peer_v_kernel(idx_ref, coef_ref, tbl_ref, x1_ref, o_ref, *, tg, picks):
    def token(t, c):
        accs = [jnp.zeros((4, LANES), F32) for _ in range(4)]
        for p in range(picks):
            lo, hi = _unpack_pair(tbl_ref[idx_ref[p, t]])
            w = coef_ref[t, p]
            s = 2 * (p % 2)
            accs[s] = accs[s] + w * lo
            accs[s + 1] = accs[s + 1] + w * hi
        upd = jnp.concatenate([accs[0] + accs[2], accs[1] + accs[3]], axis=0)
        o_ref[t] = x1_ref[t] + upd
        return c

    lax.fori_loop(0, tg, token, 0)


def _peer_v(eidx, coef, tbl, x1r, tg):
    picks, n = eidx.shape
    kern = functools.partial(_peer_v_kernel, tg=tg, picks=picks)
    return pl.pallas_call(
        kern,
        grid=(n // tg,),
        in_specs=[pl.BlockSpec((picks, tg), lambda i: (0, i), memory_space=pltpu.SMEM),
                  pl.BlockSpec((tg, picks), lambda i: (i, 0), memory_space=pltpu.SMEM),
                  pl.BlockSpec(memory_space=pltpu.VMEM),
                  pl.BlockSpec((tg, 8, LANES), lambda i: (i, 0, 0))],
        out_specs=pl.BlockSpec((tg, 8, LANES), lambda i: (i, 0, 0)),
        out_shape=jax.ShapeDtypeStruct(x1r.shape, F32),
        compiler_params=pltpu.CompilerParams(dimension_semantics=("parallel",),
                                             vmem_limit_bytes=VMEM_LIMIT),
        name="peer_v",
    )(eidx, coef, tbl, x1r)


def _pack_table(t):
    e, d = t.shape
    tb = lax.bitcast_convert_type(t.astype(BF16), jnp.uint16).astype(jnp.uint32)
    packed = tb[:, :d // 2] | (tb[:, d // 2:] << 16)
    return packed.reshape(e, d // (2 * LANES), LANES)


def _tile(n, pref):
    t = min(n, pref)
    assert n % t == 0, (n, pref)
    return t


def kernel(x, mem, attn_norm_g, w_in, conv_w, conv_b, conv_ln_g, conv_ln_b, sb_out_g, mem_norm_g, w_mem_kv, q_norm_g, k_norm_g, mem_out_g, w_out, ffn_norm_g, w_peer_q, peer_keys1, peer_keys2, peer_u, peer_v):
    batch, seq, d = x.shape
    depth = w_in.shape[0]
    conv_ch = conv_w.shape[2]
    sb_width = sb_out_g.shape[1]
    mem_width = mem_out_g.shape[1]
    assert d == 8 * LANES and conv_ch % LANES == 0 and sb_width % LANES == 0
    n = batch * seq
    q_off, k_off, v_off = 2 * conv_ch, 2 * conv_ch + sb_width, 2 * conv_ch + 2 * sb_width
    m_off = 2 * conv_ch + 3 * sb_width
    row = lambda v: v.reshape(1, -1)

    x2 = x.reshape(n, d)
    for l in range(depth):
        proj = _in_proj(x2, row(attn_norm_g[l]), w_in[l].astype(BF16), _tile(n, 512))
        conv_o = _conv(proj, conv_w[l], row(conv_b[l]), row(conv_ln_g[l]), row(conv_ln_b[l]),
                       batch, seq, _tile(seq, 512))
        sb = _stickbreak(proj, batch, seq, _tile(seq, 256), q_off // LANES, k_off // LANES,
                         v_off // LANES, sb_width // LANES)
        heads_m = mem_width // HEAD_DIM
        mk, mv = _mem_kv(mem, row(mem_norm_g[l]), w_mem_kv[l].astype(BF16),
                         row(jnp.tile(k_norm_g[l], heads_m)))
        mem_o = _mem_attn(proj, mk, mv, row(jnp.tile(q_norm_g[l], heads_m)), row(mem_out_g[l]),
                          batch, seq, _tile(seq, 512), m_off // mem_width)
        x1, h2, q = _out_proj(conv_o, sb, mem_o, x2, row(sb_out_g[l]), w_out[l].astype(BF16),
                              row(ffn_norm_g[l]), w_peer_q[l].astype(BF16), _tile(n, 512))
        eidx, gates = _peer_topk(q, peer_keys1[l].astype(BF16), peer_keys2[l].astype(BF16),
                                 _tile(n, 256))
        tg = _tile(n, 128)
        coef = _peer_u(eidx, _pack_table(peer_u[l]), h2.reshape(n, 8, LANES), gates, tg)
        x2 = _peer_v(eidx, coef, _pack_table(peer_v[l]), x1.reshape(n, 8, LANES), tg).reshape(n, d)
    return x2.reshape(batch, seq, d)
```

```python
import functools

import jax
import jax.numpy as jnp
from jax import lax
from jax.experimental import pallas as pl
from jax.experimental.pallas import tpu as pltpu

F32 = jnp.float32
BF16 = jnp.bfloat16
EPS = 1e-6

HEAD_DIM = 64
LANES = 128
CONV_HALO = 32
PEER_TOPK = 16
N_KEYS = 128
NEG_INF = float("-inf")
BIG_I32 = 2 ** 30
VMEM_LIMIT = 56 * 1024 * 1024


def _rms(x, g):
    return x * lax.rsqrt(jnp.mean(x * x, axis=-1, keepdims=True) + EPS) * g


def _dot_nt(a, b):
    return lax.dot_general(a, b, (((1,), (1,)), ((), ())), preferred_element_type=F32)


def _split_bf16(x):
    hi = x.astype(BF16)
    lo = (x - hi.astype(F32)).astype(BF16)
    return hi, lo


def _in_proj_kernel(x_ref, g_ref, w_ref, o_ref):
    h = _rms(x_ref[...], g_ref[...])
    o_ref[...] = jnp.dot(h.astype(BF16), w_ref[...], preferred_element_type=F32).astype(o_ref.dtype)


def _in_proj(x2, g, w, tm):
    n, d = x2.shape
    width = w.shape[1]
    return pl.pallas_call(
        _in_proj_kernel,
        grid=(n // tm,),
        in_specs=[pl.BlockSpec((tm, d), lambda i: (i, 0)),
                  pl.BlockSpec((1, d), lambda i: (0, 0)),
                  pl.BlockSpec((d, width), lambda i: (0, 0))],
        out_specs=pl.BlockSpec((tm, width), lambda i: (i, 0)),
        out_shape=jax.ShapeDtypeStruct((n, width), BF16),
        compiler_params=pltpu.CompilerParams(dimension_semantics=("parallel",),
                                             vmem_limit_bytes=VMEM_LIMIT),
        name="in_proj",
    )(x2, g, w)


def _conv_kernel(a_ref, gate_ref, w_ref, b_ref, lg_ref, lb_ref, o_ref, ubuf, *, tc, width):
    i = pl.program_id(1)

    @pl.when(i == 0)
    def _():
        ubuf[0:CONV_HALO, :] = jnp.zeros((CONV_HALO, ubuf.shape[1]), F32)

    @pl.when(i > 0)
    def _():
        ubuf[0:CONV_HALO, :] = ubuf[tc:tc + CONV_HALO, :]

    a = a_ref[...].astype(F32)
    gate = gate_ref[...].astype(F32)
    ubuf[CONV_HALO:CONV_HALO + tc, :] = a * jax.nn.sigmoid(gate)
    base = CONV_HALO - (width - 1)
    acc = jnp.zeros((tc, ubuf.shape[1]), F32)
    for j in range(width):
        acc = acc + w_ref[j:j + 1, :] * ubuf[base + j:base + j + tc, :]
    y = acc + b_ref[...]
    mu = jnp.mean(y, axis=-1, keepdims=True)
    yc = y - mu
    var = jnp.mean(yc * yc, axis=-1, keepdims=True)
    z = yc * lax.rsqrt(var + EPS) * lg_ref[...] + lb_ref[...]
    o_ref[...] = (z * jax.nn.sigmoid(z)).astype(o_ref.dtype)


def _conv(proj, conv_w, conv_b, ln_g, ln_b, batch, seq, tc):
    n = proj.shape[0]
    width, ch = conv_w.shape
    nt = seq // tc
    kern = functools.partial(_conv_kernel, tc=tc, width=width)
    vec = lambda: pl.BlockSpec((1, ch), lambda b, i: (0, 0))
    return pl.pallas_call(
        kern,
        grid=(batch, nt),
        in_specs=[pl.BlockSpec((tc, ch), lambda b, i: (b * nt + i, 0)),
                  pl.BlockSpec((tc, ch), lambda b, i: (b * nt + i, 1)),
                  pl.BlockSpec((width, ch), lambda b, i: (0, 0)),
                  vec(), vec(), vec()],
        out_specs=pl.BlockSpec((tc, ch), lambda b, i: (b * nt + i, 0)),
        out_shape=jax.ShapeDtypeStruct((n, ch), BF16),
        scratch_shapes=[pltpu.VMEM((tc + CONV_HALO, ch), F32)],
        compiler_params=pltpu.CompilerParams(dimension_semantics=("parallel", "arbitrary"),
                                             vmem_limit_bytes=VMEM_LIMIT),
        name="conv",
    )(proj, proj, conv_w, conv_b, ln_g, ln_b)


def _sb_kernel(q_ref, k_ref, v_ref, o_ref, acc_ref, carry_ref, *, tq):
    qi = pl.program_id(2)
    q2 = q_ref[...]
    lane = lax.broadcasted_iota(jnp.int32, (1, LANES), 1)
    row = lax.broadcasted_iota(jnp.int32, (tq, tq), 0)
    col = lax.broadcasted_iota(jnp.int32, (tq, tq), 1)
    causal = col < row
    suffix = jnp.where(row > col, -1.0, 0.0).astype(BF16)
    scale = HEAD_DIM ** -0.5

    def block(kb, h, qh, diag):
        start = pl.multiple_of(kb * tq, tq)
        kblk = k_ref[pl.ds(start, tq), :]
        vblk = v_ref[pl.ds(start, tq), :]
        z = _dot_nt(qh, kblk)
        sp = jnp.maximum(z, 0.0) + jnp.log(1.0 + jnp.exp(-jnp.abs(z)))
        if diag:
            sp = jnp.where(causal, sp, 0.0)
        after = jnp.dot(sp.astype(BF16), suffix, preferred_element_type=F32) + carry_ref[h]
        a = jnp.exp((z - sp) + after)
        if diag:
            a = jnp.where(causal, a, 0.0)
        acc_ref[h] += jnp.dot(a.astype(BF16), vblk, preferred_element_type=F32)
        carry_ref[h] -= jnp.sum(sp, axis=1, keepdims=True)

    for h in range(2):
        head = (lane >= h * HEAD_DIM) & (lane < (h + 1) * HEAD_DIM)
        qh = jnp.where(head, q2, jnp.zeros_like(q2)) * jnp.asarray(scale, q2.dtype)
        acc_ref[h] = jnp.zeros(acc_ref.shape[1:], F32)
        carry_ref[h] = jnp.zeros(carry_ref.shape[1:], F32)
        block(qi, h, qh, True)

        def body(i, c, h=h, qh=qh):
            block(qi - 1 - i, h, qh, False)
            return c

        lax.fori_loop(0, qi, body, 0)
    o_ref[...] = jnp.where(lane < HEAD_DIM, acc_ref[0], acc_ref[1]).astype(o_ref.dtype)


def _stickbreak(proj, batch, seq, tq, q_col, k_col, v_col, n_pairs):
    n = proj.shape[0]
    nq = seq // tq
    kern = functools.partial(_sb_kernel, tq=tq)
    return pl.pallas_call(
        kern,
        grid=(batch, n_pairs, nq),
        in_specs=[pl.BlockSpec((tq, LANES), lambda b, p, i: (b * nq + i, q_col + p)),
                  pl.BlockSpec((seq, LANES), lambda b, p, i: (b, k_col + p)),
                  pl.BlockSpec((seq, LANES), lambda b, p, i: (b, v_col + p))],
        out_specs=pl.BlockSpec((tq, LANES), lambda b, p, i: (b * nq + i, p)),
        out_shape=jax.ShapeDtypeStruct((n, n_pairs * LANES), F32),
        scratch_shapes=[pltpu.VMEM((2, tq, LANES), F32), pltpu.VMEM((2, tq, 1), F32)],
        compiler_params=pltpu.CompilerParams(
            dimension_semantics=("parallel", "parallel", "arbitrary"),
            vmem_limit_bytes=VMEM_LIMIT),
        name="stickbreak",
    )(proj, proj, proj)


def _head_sumsq(x, blk_ones):
    hi, lo = _split_bf16(x * x)
    return (jnp.dot(hi, blk_ones, preferred_element_type=F32)
            + jnp.dot(lo, blk_ones, preferred_element_type=F32))


def _block_ones(width):
    r = lax.broadcasted_iota(jnp.int32, (width, width), 0) // HEAD_DIM
    c = lax.broadcasted_iota(jnp.int32, (width, width), 1) // HEAD_DIM
    return jnp.where(r == c, 1.0, 0.0).astype(BF16)


def _mem_kv_kernel(mem_ref, g_ref, w_ref, kg_ref, k_ref, v_ref, *, width):
    mn = _rms(mem_ref[...], g_ref[...])
    kv = jnp.dot(mn.astype(BF16), w_ref[...], preferred_element_type=F32)
    k = kv[:, :width]
    ss = _head_sumsq(k, _block_ones(width))
    k_ref[...] = (k * lax.rsqrt(ss * (1.0 / HEAD_DIM) + EPS) * kg_ref[...]).astype(k_ref.dtype)
    v_ref[...] = kv[:, width:].astype(v_ref.dtype)


def _mem_kv(mem, g, w, kg_t):
    b, m, d = mem.shape
    width = w.shape[1] // 2
    kern = functools.partial(_mem_kv_kernel, width=width)
    out = jax.ShapeDtypeStruct((b, m, width), BF16)
    return pl.pallas_call(
        kern,
        grid=(b,),
        in_specs=[pl.BlockSpec((None, m, d), lambda i: (i, 0, 0)),
                  pl.BlockSpec((1, d), lambda i: (0, 0)),
                  pl.BlockSpec((d, 2 * width), lambda i: (0, 0)),
                  pl.BlockSpec((1, width), lambda i: (0, 0))],
        out_specs=[pl.BlockSpec((None, m, width), lambda i: (i, 0, 0)),
                   pl.BlockSpec((None, m, width), lambda i: (i, 0, 0))],
        out_shape=[out, out],
        compiler_params=pltpu.CompilerParams(dimension_semantics=("parallel",),
                                             vmem_limit_bytes=VMEM_LIMIT),
        name="mem_kv",
    )(mem, g, w, kg_t)


def _mem_attn_kernel(q_ref, k_ref, v_ref, qg_ref, og_ref, o_ref, *, width):
    q = q_ref[...].astype(F32)
    ss = _head_sumsq(q, _block_ones(width))
    qn = q * lax.rsqrt(ss * (1.0 / HEAD_DIM) + EPS) * qg_ref[...]
    k = k_ref[...]
    v = v_ref[...]
    lane = lax.broadcasted_iota(jnp.int32, (1, width), 1)
    out = jnp.zeros(q.shape, F32)
    for h in range(width // HEAD_DIM):
        head = (lane >= h * HEAD_DIM) & (lane < (h + 1) * HEAD_DIM)
        qh = jnp.where(head, qn, 0.0).astype(BF16)
        s = _dot_nt(qh, k) * (HEAD_DIM ** -0.5)
        p = jnp.exp(s - jnp.max(s, axis=-1, keepdims=True))
        o = jnp.dot(p.astype(BF16), v, preferred_element_type=F32) / jnp.sum(p, axis=-1, keepdims=True)
        out = jnp.where(head, o, out)
    o_ref[...] = _rms(out, og_ref[...]).astype(o_ref.dtype)


def _mem_attn(proj, k, v, qg_t, og, batch, seq, tm, q_col):
    n = proj.shape[0]
    _, m, width = k.shape
    nt = seq // tm
    kern = functools.partial(_mem_attn_kernel, width=width)
    return pl.pallas_call(
        kern,
        grid=(batch, nt),
        in_specs=[pl.BlockSpec((tm, width), lambda b, i: (b * nt + i, q_col)),
                  pl.BlockSpec((None, m, width), lambda b, i: (b, 0, 0)),
                  pl.BlockSpec((None, m, width), lambda b, i: (b, 0, 0)),
                  pl.BlockSpec((1, width), lambda b, i: (0, 0)),
                  pl.BlockSpec((1, width), lambda b, i: (0, 0))],
        out_specs=pl.BlockSpec((tm, width), lambda b, i: (b * nt + i, 0)),
        out_shape=jax.ShapeDtypeStruct((n, width), BF16),
        compiler_params=pltpu.CompilerParams(dimension_semantics=("parallel", "parallel"),
                                             vmem_limit_bytes=VMEM_LIMIT),
        name="mem_attn",
    )(proj, k, v, qg_t, og)


def _out_proj_kernel(conv_ref, sb_ref, mem_ref, x_ref, sbg_ref, wo_ref, fg_ref, wq_ref,
                     x1_ref, h2_ref, q_ref, *, c_w, s_w):
    sbn = _rms(sb_ref[...], sbg_ref[...]).astype(BF16)
    mixed = jnp.dot(conv_ref[...], wo_ref[0:c_w, :], preferred_element_type=F32)
    mixed = mixed + jnp.dot(sbn, wo_ref[c_w:c_w + s_w, :], preferred_element_type=F32)
    mixed = mixed + jnp.dot(mem_ref[...], wo_ref[c_w + s_w:, :], preferred_element_type=F32)
    x1 = x_ref[...] + mixed
    h2 = _rms(x1, fg_ref[...])
    x1_ref[...] = x1
    h2_ref[...] = h2
    q_ref[...] = jnp.dot(h2.astype(BF16), wq_ref[...], preferred_element_type=F32).astype(q_ref.dtype)


def _out_proj(conv_o, sb, mem_o, x2, sbg, w_out, fg, w_q, tm):
    n, d = x2.shape
    c_w, s_w, m_w = conv_o.shape[1], sb.shape[1], mem_o.shape[1]
    qw = w_q.shape[1]
    kern = functools.partial(_out_proj_kernel, c_w=c_w, s_w=s_w)
    row = lambda w: pl.BlockSpec((tm, w), lambda i: (i, 0))
    full = lambda r, c: pl.BlockSpec((r, c), lambda i: (0, 0))
    return pl.pallas_call(
        kern,
        grid=(n // tm,),
        in_specs=[row(c_w), row(s_w), row(m_w), row(d), full(1, s_w), full(c_w + s_w + m_w, d),
                  full(1, d), full(d, qw)],
        out_specs=[row(d), row(d), row(qw)],
        out_shape=[jax.ShapeDtypeStruct((n, d), F32), jax.ShapeDtypeStruct((n, d), F32),
                   jax.ShapeDtypeStruct((n, qw), BF16)],
        compiler_params=pltpu.CompilerParams(dimension_semantics=("parallel",),
                                             vmem_limit_bytes=VMEM_LIMIT),
        name="out_proj",
    )(conv_o, sb, mem_o, x2, sbg, w_out, fg, w_q)


def _topk_rows(s, ids, k):
    vals, sel = [], []
    for _ in range(k):
        m = jnp.max(s, axis=0, keepdims=True)
        i = jnp.min(jnp.where(s == m, ids, BIG_I32), axis=0, keepdims=True)
        s = jnp.where(ids == i, NEG_INF, s)
        vals.append(m)
        sel.append(i)
    return jnp.concatenate(vals, axis=0), jnp.concatenate(sel, axis=0)


def _peer_topk_kernel(q_ref, k1_ref, k2_ref, eidx_ref, gates_ref, *, half):
    t = q_ref.shape[0]
    key_ids = lax.broadcasted_iota(jnp.int32, (N_KEYS, t), 0)
    q = q_ref[...]
    v1, i1 = _topk_rows(_dot_nt(k1_ref[...], q[:, :half]), key_ids, PEER_TOPK)
    v2, i2 = _topk_rows(_dot_nt(k2_ref[...], q[:, half:]), key_ids, PEER_TOPK)
    cand_s, cand_i = [v1[0:1] + v2], [i1[0:1] * N_KEYS + i2]
    sub = lax.broadcasted_iota(jnp.int32, (8, t), 0)
    for a in range(1, 8):
        ok = sub < (PEER_TOPK // (a + 1))
        cand_s.append(jnp.where(ok, v1[a:a + 1] + v2[0:8], NEG_INF))
        cand_i.append(i1[a:a + 1] * N_KEYS + i2[0:8])
    cand_s.append(v1[8:16] + v2[0:1])
    cand_i.append(i1[8:16] * N_KEYS + i2[0:1])
    top_s, eidx = _topk_rows(jnp.concatenate(cand_s, axis=0), jnp.concatenate(cand_i, axis=0), PEER_TOPK)
    p = jnp.exp(top_s - top_s[0:1])
    eidx_ref[...] = eidx
    gates_ref[...] = p / jnp.sum(p, axis=0, keepdims=True)


def _peer_topk(q, k1, k2, tt):
    n, qw = q.shape
    heads, n_keys, half = k1.shape
    kern = functools.partial(_peer_topk_kernel, half=half)
    out_spec = lambda: pl.BlockSpec((PEER_TOPK, tt), lambda i, h: (h, i))
    return pl.pallas_call(
        kern,
        grid=(n // tt, heads),
        in_specs=[pl.BlockSpec((tt, 2 * half), lambda i, h: (i, h)),
                  pl.BlockSpec((None, n_keys, half), lambda i, h: (h, 0, 0)),
                  pl.BlockSpec((None, n_keys, half), lambda i, h: (h, 0, 0))],
        out_specs=[out_spec(), out_spec()],
        out_shape=[jax.ShapeDtypeStruct((heads * PEER_TOPK, n), jnp.int32),
                   jax.ShapeDtypeStruct((heads * PEER_TOPK, n), F32)],
        compiler_params=pltpu.CompilerParams(dimension_semantics=("parallel", "parallel"),
                                             vmem_limit_bytes=VMEM_LIMIT),
        name="peer_topk",
    )(q, k1, k2)


def _unpack_pair(w):
    lo = pltpu.bitcast(w << 16, F32)
    hi = pltpu.bitcast(w & jnp.uint32(0xFFFF0000), F32)
    return lo, hi


def _peer_u_kernel(idx_ref, tbl_ref, h2_ref, gates_ref, coef_ref, y0_ref, y1_ref, act_ref, *, tg, picks):
    lane = lax.broadcasted_iota(jnp.int32, (1, tg), 1)

    def products(t, y_ref):
        hv = h2_ref[t]
        h_lo, h_hi = hv[0:4], hv[4:8]
        for p in range(picks):
            lo, hi = _unpack_pair(tbl_ref[idx_ref[t, p]])
            y_ref[4 * p:4 * p + 4, :] = lo * h_lo + hi * h_hi

    def reduce(t, y_ref):
        part = (y_ref[pl.ds(0, picks, stride=4), :] + y_ref[pl.ds(1, picks, stride=4), :]
                + y_ref[pl.ds(2, picks, stride=4), :] + y_ref[pl.ds(3, picks, stride=4), :])
        act = jnp.sum(part, axis=1, keepdims=True)
        act_ref[...] = jnp.where(lane == t, act, act_ref[...])

    y1_ref[...] = jnp.zeros(y1_ref.shape, F32)
    act_ref[...] = jnp.zeros(act_ref.shape, F32)

    def pair(j, c):
        t0 = 2 * j
        products(t0, y0_ref)
        reduce(t0 - 1, y1_ref)
        products(t0 + 1, y1_ref)
        reduce(t0, y0_ref)
        return c

    lax.fori_loop(0, tg // 2, pair, 0)
    reduce(tg - 1, y1_ref)
    act = act_ref[...]
    gelu = 0.5 * act * (1.0 + lax.erf(act * (2.0 ** -0.5)))
    coef_ref[...] = gates_ref[...] * gelu


def _peer_u(eidx, tbl, h2r, gates, tg):
    n, picks = eidx.shape
    kern = functools.partial(_peer_u_kernel, tg=tg, picks=picks)
    return pl.pallas_call(
        kern,
        grid=(n // tg,),
        in_specs=[pl.BlockSpec((tg, picks), lambda i: (i, 0), memory_space=pltpu.SMEM,
                               pipeline_mode=pl.Buffered(1)),
                  pl.BlockSpec(memory_space=pltpu.VMEM),
                  pl.BlockSpec((tg, 8, LANES), lambda i: (i, 0, 0)),
                  pl.BlockSpec((picks, tg), lambda i: (0, i))],
        out_specs=pl.BlockSpec((picks, tg), lambda i: (0, i)),
        out_shape=jax.ShapeDtypeStruct((picks, n), F32),
        scratch_shapes=[pltpu.VMEM((4 * picks, LANES), F32), pltpu.VMEM((4 * picks, LANES), F32),
                        pltpu.VMEM((picks, tg), F32)],
        compiler_params=pltpu.CompilerParams(dimension_semantics=("parallel",),
                                             vmem_limit_bytes=VMEM_LIMIT),
        name="peer_u",
    )(eidx, tbl, h2r, gates)


def _peer_v_kernel(idx_ref, coef_ref, tbl_ref, x1_ref, o_ref, chi_ref, clo_ref, m0_ref, m1_ref,
                   *, tg, picks):
    hi, lo = _split_bf16(coef_ref[...])
    chi_ref[...] = hi
    clo_ref[...] = lo
    tok = lax.broadcasted_iota(jnp.int32, (tg, LANES), 0)

    def spread(t, m_ref):
        onehot = jnp.where(tok == t, 1.0, 0.0).astype(BF16)
        m_ref[...] = (jnp.dot(chi_ref[...], onehot, preferred_element_type=F32)
                      + jnp.dot(clo_ref[...], onehot, preferred_element_type=F32))

    def combine(t, m_ref):
        accs = [jnp.zeros((4, LANES), F32) for _ in range(4)]
        for p in range(picks):
            lo, hi = _unpack_pair(tbl_ref[idx_ref[t, p]])
            w = m_ref[p:p + 1, :]
            s = 2 * (p % 2)
            accs[s] = accs[s] + w * lo
            accs[s + 1] = accs[s + 1] + w * hi
        upd = jnp.concatenate([accs[0] + accs[2], accs[1] + accs[3]], axis=0)
        o_ref[t] = x1_ref[t] + upd

    spread(0, m0_ref)

    def pair(j, c):
        t0 = 2 * j
        spread(t0 + 1, m1_ref)
        combine(t0, m0_ref)
        spread(t0 + 2, m0_ref)
        combine(t0 + 1, m1_ref)
        return c

    lax.fori_loop(0, tg // 2, pair, 0)


def _peer_v(eidx, coef, tbl, x1r, tg):
    n, picks = eidx.shape
    kern = functools.partial(_peer_v_kernel, tg=tg, picks=picks)
    return pl.pallas_call(
        kern,
        grid=(n // tg,),
        in_specs=[pl.BlockSpec((tg, picks), lambda i: (i, 0), memory_space=pltpu.SMEM,
                               pipeline_mode=pl.Buffered(1)),
                  pl.BlockSpec((picks, tg), lambda i: (0, i)),
                  pl.BlockSpec(memory_space=pltpu.VMEM),
                  pl.BlockSpec((tg, 8, LANES), lambda i: (i, 0, 0))],
        out_specs=pl.BlockSpec((tg, 8, LANES), lambda i: (i, 0, 0)),
        out_shape=jax.ShapeDtypeStruct(x1r.shape, F32),
        scratch_shapes=[pltpu.VMEM((picks, tg), BF16), pltpu.VMEM((picks, tg), BF16),
                        pltpu.VMEM((picks, LANES), F32), pltpu.VMEM((picks, LANES), F32)],
        compiler_params=pltpu.CompilerParams(dimension_semantics=("parallel",),
                                             vmem_limit_bytes=VMEM_LIMIT),
        name="peer_v",
    )(eidx, coef, tbl, x1r)


def _pack_table(t):
    e, d = t.shape
    tb = lax.bitcast_convert_type(t.astype(BF16), jnp.uint16).astype(jnp.uint32)
    packed = tb[:, :d // 2] | (tb[:, d // 2:] << 16)
    return packed.reshape(e, d // (2 * LANES), LANES)


def _tile(n, pref):
    t = min(n, pref)
    assert n % t == 0, (n, pref)
    return t


def kernel(x, mem, attn_norm_g, w_in, conv_w, conv_b, conv_ln_g, conv_ln_b, sb_out_g, mem_norm_g, w_mem_kv, q_norm_g, k_norm_g, mem_out_g, w_out, ffn_norm_g, w_peer_q, peer_keys1, peer_keys2, peer_u, peer_v):
    batch, seq, d = x.shape
    depth = w_in.shape[0]
    conv_ch = conv_w.shape[2]
    sb_width = sb_out_g.shape[1]
    mem_width = mem_out_g.shape[1]
    assert d == 8 * LANES and conv_ch % LANES == 0 and sb_width % LANES == 0
    n = batch * seq
    q_off, k_off, v_off = 2 * conv_ch, 2 * conv_ch + sb_width, 2 * conv_ch + 2 * sb_width
    m_off = 2 * conv_ch + 3 * sb_width
    row = lambda v: v.reshape(1, -1)

    x2 = x.reshape(n, d)
    for l in range(depth):
        proj = _in_proj(x2, row(attn_norm_g[l]), w_in[l].astype(BF16), _tile(n, 512))
        conv_o = _conv(proj, conv_w[l], row(conv_b[l]), row(conv_ln_g[l]), row(conv_ln_b[l]),
                       batch, seq, _tile(seq, 512))
        sb = _stickbreak(proj, batch, seq, _tile(seq, 256), q_off // LANES, k_off // LANES,
                         v_off // LANES, sb_width // LANES)
        heads_m = mem_width // HEAD_DIM
        mk, mv = _mem_kv(mem, row(mem_norm_g[l]), w_mem_kv[l].astype(BF16),
                         row(jnp.tile(k_norm_g[l], heads_m)))
        mem_o = _mem_attn(proj, mk, mv, row(jnp.tile(q_norm_g[l], heads_m)), row(mem_out_g[l]),
                          batch, seq, _tile(seq, 512), m_off // mem_width)
        x1, h2, q = _out_proj(conv_o, sb, mem_o, x2, row(sb_out_g[l]), w_out[l].astype(BF16),
                              row(ffn_norm_g[l]), w_peer_q[l].astype(BF16), _tile(n, 512))
        eidx, gates = _peer_topk(q, peer_keys1[l].astype(BF16), peer_keys2[l].astype(BF16),
                                 _tile(n, 256))
        tg = _tile(n, 128)
        eidx_t = eidx.T
        coef = _peer_u(eidx_t, _pack_table(peer_u[l]), h2.reshape(n, 8, LANES), gates, tg)
        x2 = _peer_v(eidx_t, coef, _pack_table(peer_v[l]), x1.reshape(n, 8, LANES), tg).reshape(n, d)
    return x2.reshape(batch, seq, d)
```

```python
import functools

import jax
import jax.numpy as jnp
from jax import lax
from jax.experimental import pallas as pl
from jax.experimental.pallas import tpu as pltpu

F32 = jnp.float32
BF16 = jnp.bfloat16
EPS = 1e-6

HEAD_DIM = 64
LANES = 128
CONV_HALO = 32
SB_ROWS = 128
PEER_V_GROUP = 4
PEER_U_GROUP = 4
PEER_TOPK = 16
N_KEYS = 128
NEG_INF = float("-inf")
BIG_ID = float(2 ** 24)
VMEM_LIMIT = 56 * 1024 * 1024


def _rms(x, g):
    return x * lax.rsqrt(jnp.mean(x * x, axis=-1, keepdims=True) + EPS) * g


def _dot_nt(a, b):
    return lax.dot_general(a, b, (((1,), (1,)), ((), ())), preferred_element_type=F32)


def _split_bf16(x):
    hi = x.astype(BF16)
    lo = (x - hi.astype(F32)).astype(BF16)
    return hi, lo


def _in_proj_kernel(x_ref, g_ref, w_ref, o_ref):
    h = _rms(x_ref[...], g_ref[...])
    o_ref[...] = jnp.dot(h.astype(BF16), w_ref[...], preferred_element_type=F32).astype(o_ref.dtype)


def _in_proj(x2, g, w, tm):
    n, d = x2.shape
    width = w.shape[1]
    return pl.pallas_call(
        _in_proj_kernel,
        grid=(n // tm,),
        in_specs=[pl.BlockSpec((tm, d), lambda i: (i, 0)),
                  pl.BlockSpec((1, d), lambda i: (0, 0)),
                  pl.BlockSpec((d, width), lambda i: (0, 0))],
        out_specs=pl.BlockSpec((tm, width), lambda i: (i, 0)),
        out_shape=jax.ShapeDtypeStruct((n, width), BF16),
        compiler_params=pltpu.CompilerParams(dimension_semantics=("parallel",),
                                             vmem_limit_bytes=VMEM_LIMIT),
        name="in_proj",
    )(x2, g, w)


def _conv_kernel(a_ref, gate_ref, w_ref, b_ref, lg_ref, lb_ref, o_ref, ubuf, *, tc, width):
    i = pl.program_id(1)

    @pl.when(i == 0)
    def _():
        ubuf[0:CONV_HALO, :] = jnp.zeros((CONV_HALO, ubuf.shape[1]), F32)

    @pl.when(i > 0)
    def _():
        ubuf[0:CONV_HALO, :] = ubuf[tc:tc + CONV_HALO, :]

    a = a_ref[...].astype(F32)
    gate = gate_ref[...].astype(F32)
    ubuf[CONV_HALO:CONV_HALO + tc, :] = a * jax.nn.sigmoid(gate)
    base = CONV_HALO - (width - 1)
    acc = jnp.zeros((tc, ubuf.shape[1]), F32)
    for j in range(width):
        acc = acc + w_ref[j:j + 1, :] * ubuf[base + j:base + j + tc, :]
    y = acc + b_ref[...]
    mu = jnp.mean(y, axis=-1, keepdims=True)
    yc = y - mu
    var = jnp.mean(yc * yc, axis=-1, keepdims=True)
    z = yc * lax.rsqrt(var + EPS) * lg_ref[...] + lb_ref[...]
    o_ref[...] = (z * jax.nn.sigmoid(z)).astype(o_ref.dtype)


def _conv(proj, conv_w, conv_b, ln_g, ln_b, batch, seq, tc):
    n = proj.shape[0]
    width, ch = conv_w.shape
    nt = seq // tc
    kern = functools.partial(_conv_kernel, tc=tc, width=width)
    vec = lambda: pl.BlockSpec((1, ch), lambda b, i: (0, 0))
    return pl.pallas_call(
        kern,
        grid=(batch, nt),
        in_specs=[pl.BlockSpec((tc, ch), lambda b, i: (b * nt + i, 0)),
                  pl.BlockSpec((tc, ch), lambda b, i: (b * nt + i, 1)),
                  pl.BlockSpec((width, ch), lambda b, i: (0, 0)),
                  vec(), vec(), vec()],
        out_specs=pl.BlockSpec((tc, ch), lambda b, i: (b * nt + i, 0)),
        out_shape=jax.ShapeDtypeStruct((n, ch), BF16),
        scratch_shapes=[pltpu.VMEM((tc + CONV_HALO, ch), F32)],
        compiler_params=pltpu.CompilerParams(dimension_semantics=("parallel", "arbitrary"),
                                             vmem_limit_bytes=VMEM_LIMIT),
        name="conv",
    )(proj, proj, conv_w, conv_b, ln_g, ln_b)


def _sb_kernel(q_ref, k_ref, v_ref, o_ref, *scratch, tq):
    n_chain = len(scratch) // 2
    acc_refs, carry_refs = scratch[:n_chain], scratch[n_chain:]
    qi = pl.program_id(2)
    q2 = q_ref[...]
    lane = lax.broadcasted_iota(jnp.int32, (1, LANES), 1)
    row = lax.broadcasted_iota(jnp.int32, (tq, tq), 0)
    col = lax.broadcasted_iota(jnp.int32, (tq, tq), 1)
    suffix = jnp.where(row > col, -1.0, 0.0).astype(BF16)
    scale = jnp.asarray(HEAD_DIM ** -0.5, q2.dtype)
    chains = []
    for h in range(2):
        head = (lane >= h * HEAD_DIM) & (lane < (h + 1) * HEAD_DIM)
        qh = jnp.where(head, q2, jnp.zeros_like(q2)) * scale
        for r0 in range(0, tq, SB_ROWS):
            chains.append((qh[r0:r0 + SB_ROWS], r0))
    for c in range(len(chains)):
        acc_refs[c][...] = jnp.zeros(acc_refs[c].shape, F32)
        carry_refs[c][...] = jnp.zeros(carry_refs[c].shape, F32)

    def stages(kb, diag):
        st = {}
        n = len(chains)

        def qk():
            start = pl.multiple_of(kb * tq, tq)
            st["k"] = k_ref[pl.ds(start, tq), :]
            st["v"] = v_ref[pl.ds(start, tq), :]
            st["z"] = [_dot_nt(qc, st["k"]) for qc, _ in chains]

        def logs():
            st["lg"], st["spb"], st["keep"] = [], [], []
            for z, (_, r0) in zip(st["z"], chains):
                lg = jnp.minimum(z, 0.0) - jnp.log(1.0 + jnp.exp(jnp.minimum(z, -z)))
                sp = z - lg
                if diag:
                    keep = (lax.broadcasted_iota(jnp.int32, (SB_ROWS, tq), 1)
                            < lax.broadcasted_iota(jnp.int32, (SB_ROWS, tq), 0) + r0)
                    sp = jnp.where(keep, sp, 0.0)
                    st["keep"].append(keep)
                st["lg"].append(lg)
                st["spb"].append(sp.astype(BF16))

        def sums():
            st["r"] = [jnp.dot(spb, suffix, preferred_element_type=F32) for spb in st["spb"]]

        def weights():
            st["a"] = []
            for c in range(n):
                after = st["r"][c] + carry_refs[c][...]
                a = jnp.exp(st["lg"][c] + after)
                if diag:
                    a = jnp.where(st["keep"][c], a, 0.0)
                st["a"].append(a.astype(BF16))
                carry_refs[c][...] = after[:, 0:1] - st["spb"][c][:, 0:LANES].astype(F32)[:, 0:1]

        def values():
            for c in range(n):
                acc_refs[c][...] += jnp.dot(st["a"][c], st["v"], preferred_element_type=F32)

        return [qk, logs, sums, weights, values]

    def run(blocks):
        seqs = [stages(kb, diag) for kb, diag in blocks]
        depth = len(seqs[0])
        for step in range(depth + len(seqs) - 1):
            for i, sq in enumerate(seqs):
                if 0 <= step - i < depth:
                    sq[step - i]()

    run([(qi, True)])

    def pair(i, c):
        run([(qi - 1 - 2 * i, False), (qi - 2 - 2 * i, False)])
        return c

    lax.fori_loop(0, qi // 2, pair, 0)

    @pl.when(qi % 2 == 1)
    def _():
        run([(0, False)])

    n_sub = tq // SB_ROWS
    out = [jnp.concatenate([acc_refs[h * n_sub + s][...] for s in range(n_sub)], axis=0) for h in range(2)]
    o_ref[...] = jnp.where(lane < HEAD_DIM, out[0], out[1]).astype(o_ref.dtype)


def _stickbreak(proj, batch, seq, tq, q_col, k_col, v_col, n_pairs):
    n = proj.shape[0]
    nq = seq // tq
    kern = functools.partial(_sb_kernel, tq=tq)
    return pl.pallas_call(
        kern,
        grid=(batch, n_pairs, nq),
        in_specs=[pl.BlockSpec((tq, LANES), lambda b, p, i: (b * nq + i, q_col + p)),
                  pl.BlockSpec((seq, LANES), lambda b, p, i: (b, k_col + p)),
                  pl.BlockSpec((seq, LANES), lambda b, p, i: (b, v_col + p))],
        out_specs=pl.BlockSpec((tq, LANES), lambda b, p, i: (b * nq + i, p)),
        out_shape=jax.ShapeDtypeStruct((n, n_pairs * LANES), F32),
        scratch_shapes=([pltpu.VMEM((SB_ROWS, LANES), F32)] * (2 * tq // SB_ROWS)
                        + [pltpu.VMEM((SB_ROWS, 1), F32)] * (2 * tq // SB_ROWS)),
        compiler_params=pltpu.CompilerParams(
            dimension_semantics=("parallel", "parallel", "arbitrary"),
            vmem_limit_bytes=VMEM_LIMIT),
        name="stickbreak",
    )(proj, proj, proj)


def _head_sumsq(x, blk_ones):
    hi, lo = _split_bf16(x * x)
    return (jnp.dot(hi, blk_ones, preferred_element_type=F32)
            + jnp.dot(lo, blk_ones, preferred_element_type=F32))


def _block_ones(width):
    r = lax.broadcasted_iota(jnp.int32, (width, width), 0) // HEAD_DIM
    c = lax.broadcasted_iota(jnp.int32, (width, width), 1) // HEAD_DIM
    return jnp.where(r == c, 1.0, 0.0).astype(BF16)


def _mem_kv_kernel(mem_ref, g_ref, w_ref, kg_ref, k_ref, v_ref, *, width):
    mn = _rms(mem_ref[...], g_ref[...])
    kv = jnp.dot(mn.astype(BF16), w_ref[...], preferred_element_type=F32)
    k = kv[:, :width]
    ss = _head_sumsq(k, _block_ones(width))
    k_ref[...] = (k * lax.rsqrt(ss * (1.0 / HEAD_DIM) + EPS) * kg_ref[...]).astype(k_ref.dtype)
    v_ref[...] = kv[:, width:].astype(v_ref.dtype)


def _mem_kv(mem, g, w, kg_t):
    b, m, d = mem.shape
    width = w.shape[1] // 2
    kern = functools.partial(_mem_kv_kernel, width=width)
    out = jax.ShapeDtypeStruct((b, m, width), BF16)
    return pl.pallas_call(
        kern,
        grid=(b,),
        in_specs=[pl.BlockSpec((None, m, d), lambda i: (i, 0, 0)),
                  pl.BlockSpec((1, d), lambda i: (0, 0)),
                  pl.BlockSpec((d, 2 * width), lambda i: (0, 0)),
                  pl.BlockSpec((1, width), lambda i: (0, 0))],
        out_specs=[pl.BlockSpec((None, m, width), lambda i: (i, 0, 0)),
                   pl.BlockSpec((None, m, width), lambda i: (i, 0, 0))],
        out_shape=[out, out],
        compiler_params=pltpu.CompilerParams(dimension_semantics=("parallel",),
                                             vmem_limit_bytes=VMEM_LIMIT),
        name="mem_kv",
    )(mem, g, w, kg_t)


def _mem_attn_kernel(q_ref, k_ref, v_ref, qg_ref, og_ref, o_ref, *, width):
    q = q_ref[...].astype(F32)
    ss = _head_sumsq(q, _block_ones(width))
    qn = q * lax.rsqrt(ss * (1.0 / HEAD_DIM) + EPS) * qg_ref[...]
    k = k_ref[...]
    v = v_ref[...]
    lane = lax.broadcasted_iota(jnp.int32, (1, width), 1)
    out = jnp.zeros(q.shape, F32)
    for h in range(width // HEAD_DIM):
        head = (lane >= h * HEAD_DIM) & (lane < (h + 1) * HEAD_DIM)
        qh = jnp.where(head, qn, 0.0).astype(BF16)
        s = _dot_nt(qh, k) * (HEAD_DIM ** -0.5)
        p = jnp.exp(s - jnp.max(s, axis=-1, keepdims=True))
        o = jnp.dot(p.astype(BF16), v, preferred_element_type=F32) / jnp.sum(p, axis=-1, keepdims=True)
        out = jnp.where(head, o, out)
    o_ref[...] = _rms(out, og_ref[...]).astype(o_ref.dtype)


def _mem_attn(proj, k, v, qg_t, og, batch, seq, tm, q_col):
    n = proj.shape[0]
    _, m, width = k.shape
    nt = seq // tm
    kern = functools.partial(_mem_attn_kernel, width=width)
    return pl.pallas_call(
        kern,
        grid=(batch, nt),
        in_specs=[pl.BlockSpec((tm, width), lambda b, i: (b * nt + i, q_col)),
                  pl.BlockSpec((None, m, width), lambda b, i: (b, 0, 0)),
                  pl.BlockSpec((None, m, width), lambda b, i: (b, 0, 0)),
                  pl.BlockSpec((1, width), lambda b, i: (0, 0)),
                  pl.BlockSpec((1, width), lambda b, i: (0, 0))],
        out_specs=pl.BlockSpec((tm, width), lambda b, i: (b * nt + i, 0)),
        out_shape=jax.ShapeDtypeStruct((n, width), BF16),
        compiler_params=pltpu.CompilerParams(dimension_semantics=("parallel", "parallel"),
                                             vmem_limit_bytes=VMEM_LIMIT),
        name="mem_attn",
    )(proj, k, v, qg_t, og)


def _out_proj_kernel(conv_ref, sb_ref, mem_ref, x_ref, sbg_ref, wo_ref, fg_ref, wq_ref,
                     x1_ref, h2_ref, q_ref, *, c_w, s_w):
    sbn = _rms(sb_ref[...], sbg_ref[...]).astype(BF16)
    mixed = jnp.dot(conv_ref[...], wo_ref[0:c_w, :], preferred_element_type=F32)
    mixed = mixed + jnp.dot(sbn, wo_ref[c_w:c_w + s_w, :], preferred_element_type=F32)
    mixed = mixed + jnp.dot(mem_ref[...], wo_ref[c_w + s_w:, :], preferred_element_type=F32)
    x1 = x_ref[...] + mixed
    h2 = _rms(x1, fg_ref[...])
    x1_ref[...] = x1
    h2_ref[...] = h2
    q_ref[...] = jnp.dot(h2.astype(BF16), wq_ref[...], preferred_element_type=F32).astype(q_ref.dtype)


def _out_proj(conv_o, sb, mem_o, x2, sbg, w_out, fg, w_q, tm):
    n, d = x2.shape
    c_w, s_w, m_w = conv_o.shape[1], sb.shape[1], mem_o.shape[1]
    qw = w_q.shape[1]
    kern = functools.partial(_out_proj_kernel, c_w=c_w, s_w=s_w)
    row = lambda w: pl.BlockSpec((tm, w), lambda i: (i, 0))
    full = lambda r, c: pl.BlockSpec((r, c), lambda i: (0, 0))
    return pl.pallas_call(
        kern,
        grid=(n // tm,),
        in_specs=[row(c_w), row(s_w), row(m_w), row(d), full(1, s_w), full(c_w + s_w + m_w, d),
                  full(1, d), full(d, qw)],
        out_specs=[row(d), row(d), row(qw)],
        out_shape=[jax.ShapeDtypeStruct((n, d), F32), jax.ShapeDtypeStruct((n, d), F32),
                   jax.ShapeDtypeStruct((n, qw), BF16)],
        compiler_params=pltpu.CompilerParams(dimension_semantics=("parallel",),
                                             vmem_limit_bytes=VMEM_LIMIT),
        name="out_proj",
    )(conv_o, sb, mem_o, x2, sbg, w_out, fg, w_q)


def _topk_rows(s, ids, k):
    vals, sel = [], []
    for _ in range(k):
        m = jnp.max(s, axis=0, keepdims=True)
        i = jnp.min(jnp.where(s == m, ids, BIG_ID), axis=0, keepdims=True)
        s = jnp.where(ids == i, NEG_INF, s)
        vals.append(m)
        sel.append(i)
    return jnp.concatenate(vals, axis=0), jnp.concatenate(sel, axis=0)


def _peer_topk_kernel(q_ref, k1_ref, k2_ref, eidx_ref, gates_ref, *, half):
    t = q_ref.shape[0]
    key_ids = lax.broadcasted_iota(jnp.int32, (N_KEYS, t), 0).astype(F32)
    q = q_ref[...]
    v1, i1 = _topk_rows(_dot_nt(k1_ref[...], q[:, :half]), key_ids, PEER_TOPK)
    v2, i2 = _topk_rows(_dot_nt(k2_ref[...], q[:, half:]), key_ids, PEER_TOPK)
    cand_s, cand_i = [v1[0:1] + v2], [i1[0:1] * N_KEYS + i2]
    sub = lax.broadcasted_iota(jnp.int32, (8, t), 0)
    for a in range(1, 8):
        ok = sub < (PEER_TOPK // (a + 1))
        cand_s.append(jnp.where(ok, v1[a:a + 1] + v2[0:8], NEG_INF))
        cand_i.append(i1[a:a + 1] * N_KEYS + i2[0:8])
    cand_s.append(v1[8:16] + v2[0:1])
    cand_i.append(i1[8:16] * N_KEYS + i2[0:1])
    top_s, eidx = _topk_rows(jnp.concatenate(cand_s, axis=0), jnp.concatenate(cand_i, axis=0), PEER_TOPK)
    p = jnp.exp(top_s - top_s[0:1])
    eidx_ref[...] = eidx.astype(jnp.int32)
    gates_ref[...] = p / jnp.sum(p, axis=0, keepdims=True)


def _peer_topk(q, k1, k2, tt):
    n, qw = q.shape
    heads, n_keys, half = k1.shape
    kern = functools.partial(_peer_topk_kernel, half=half)
    out_spec = lambda: pl.BlockSpec((PEER_TOPK, tt), lambda i, h: (h, i))
    return pl.pallas_call(
        kern,
        grid=(n // tt, heads),
        in_specs=[pl.BlockSpec((tt, 2 * half), lambda i, h: (i, h)),
                  pl.BlockSpec((None, n_keys, half), lambda i, h: (h, 0, 0)),
                  pl.BlockSpec((None, n_keys, half), lambda i, h: (h, 0, 0))],
        out_specs=[out_spec(), out_spec()],
        out_shape=[jax.ShapeDtypeStruct((heads * PEER_TOPK, n), jnp.int32),
                   jax.ShapeDtypeStruct((heads * PEER_TOPK, n), F32)],
        compiler_params=pltpu.CompilerParams(dimension_semantics=("parallel", "parallel"),
                                             vmem_limit_bytes=VMEM_LIMIT),
        name="peer_topk",
    )(q, k1, k2)


def _unpack_pair(w):
    lo = pltpu.bitcast(w << 16, F32)
    hi = pltpu.bitcast(w & jnp.uint32(0xFFFF0000), F32)
    return lo, hi


def _peer_u_kernel(idx_ref, tbl_ref, h2_ref, gates_ref, coef_ref, act_ref, *y_refs, tg, picks):
    lane = lax.broadcasted_iota(jnp.int32, (1, tg), 1)
    group = len(y_refs)

    def products(t, y_ref):
        hv = h2_ref[t]
        h_lo, h_hi = hv[0:4], hv[4:8]
        for p in range(picks):
            lo, hi = _unpack_pair(tbl_ref[idx_ref[t, p]])
            y_ref[4 * p:4 * p + 4, :] = lo * h_lo + hi * h_hi

    def reduce(t, y_ref):
        part = (y_ref[pl.ds(0, picks, stride=4), :] + y_ref[pl.ds(1, picks, stride=4), :]
                + y_ref[pl.ds(2, picks, stride=4), :] + y_ref[pl.ds(3, picks, stride=4), :])
        act = jnp.sum(part, axis=1, keepdims=True)
        act_ref[...] = jnp.where(lane == t, act, act_ref[...])

    y_refs[-1][...] = jnp.zeros(y_refs[-1].shape, F32)
    act_ref[...] = jnp.zeros(act_ref.shape, F32)

    def body(j, c):
        t0 = group * j
        for k in range(group):
            products(t0 + k, y_refs[k])
            reduce(t0 + k - 1, y_refs[k - 1])
        return c

    lax.fori_loop(0, tg // group, body, 0)
    reduce(tg - 1, y_refs[-1])
    act = act_ref[...]
    gelu = 0.5 * act * (1.0 + lax.erf(act * (2.0 ** -0.5)))
    coef_ref[...] = gates_ref[...] * gelu


def _peer_u(eidx, tbl, h2r, gates, tg):
    n, picks = eidx.shape
    kern = functools.partial(_peer_u_kernel, tg=tg, picks=picks)
    return pl.pallas_call(
        kern,
        grid=(n // tg,),
        in_specs=[pl.BlockSpec((tg, picks), lambda i: (i, 0), memory_space=pltpu.SMEM,
                               pipeline_mode=pl.Buffered(1)),
                  pl.BlockSpec(memory_space=pltpu.VMEM),
                  pl.BlockSpec((tg, 8, LANES), lambda i: (i, 0, 0)),
                  pl.BlockSpec((picks, tg), lambda i: (0, i))],
        out_specs=pl.BlockSpec((picks, tg), lambda i: (0, i)),
        out_shape=jax.ShapeDtypeStruct((picks, n), F32),
        scratch_shapes=([pltpu.VMEM((picks, tg), F32)]
                        + [pltpu.VMEM((4 * picks, LANES), F32)] * PEER_U_GROUP),
        compiler_params=pltpu.CompilerParams(dimension_semantics=("parallel",),
                                             vmem_limit_bytes=VMEM_LIMIT),
        name="peer_u",
    )(eidx, tbl, h2r, gates)


def _peer_v_kernel(idx_ref, coef_ref, tbl_ref, x1_ref, o_ref, chi_ref, clo_ref, *m_refs, tg, picks):
    hi, lo = _split_bf16(coef_ref[...])
    chi_ref[...] = hi
    clo_ref[...] = lo
    tok = lax.broadcasted_iota(jnp.int32, (tg, LANES), 0)

    def spread(t, m_ref):
        onehot = jnp.where(tok == t, 1.0, 0.0).astype(BF16)
        m_ref[...] = (jnp.dot(chi_ref[...], onehot, preferred_element_type=F32)
                      + jnp.dot(clo_ref[...], onehot, preferred_element_type=F32))

    def combine(t, m_ref):
        accs = [jnp.zeros((4, LANES), F32) for _ in range(4)]
        for p in range(picks):
            lo, hi = _unpack_pair(tbl_ref[idx_ref[t, p]])
            w = m_ref[p:p + 1, :]
            s = 2 * (p % 2)
            accs[s] = accs[s] + w * lo
            accs[s + 1] = accs[s + 1] + w * hi
        upd = jnp.concatenate([accs[0] + accs[2], accs[1] + accs[3]], axis=0)
        o_ref[t] = x1_ref[t] + upd

    group = len(m_refs)
    for k, m_ref in enumerate(m_refs):
        spread(k, m_ref)

    def body(j, c):
        t0 = group * j
        for k, m_ref in enumerate(m_refs):
            combine(t0 + k, m_ref)
            spread(t0 + group + k, m_ref)
        return c

    lax.fori_loop(0, tg // group, body, 0)


def _peer_v(eidx, coef, tbl, x1r, tg):
    n, picks = eidx.shape
    kern = functools.partial(_peer_v_kernel, tg=tg, picks=picks)
    return pl.pallas_call(
        kern,
        grid=(n // tg,),
        in_specs=[pl.BlockSpec((tg, picks), lambda i: (i, 0), memory_space=pltpu.SMEM,
                               pipeline_mode=pl.Buffered(1)),
                  pl.BlockSpec((picks, tg), lambda i: (0, i)),
                  pl.BlockSpec(memory_space=pltpu.VMEM),
                  pl.BlockSpec((tg, 8, LANES), lambda i: (i, 0, 0))],
        out_specs=pl.BlockSpec((tg, 8, LANES), lambda i: (i, 0, 0)),
        out_shape=jax.ShapeDtypeStruct(x1r.shape, F32),
        scratch_shapes=([pltpu.VMEM((picks, tg), BF16)] * 2
                        + [pltpu.VMEM((picks, LANES), F32)] * PEER_V_GROUP),
        compiler_params=pltpu.CompilerParams(dimension_semantics=("parallel",),
                                             vmem_limit_bytes=VMEM_LIMIT),
        name="peer_v",
    )(eidx, coef, tbl, x1r)


def _pack_table(t):
    e, d = t.shape
    tb = lax.bitcast_convert_type(t.astype(BF16), jnp.uint16).astype(jnp.uint32)
    packed = tb[:, :d // 2] | (tb[:, d // 2:] << 16)
    return packed.reshape(e, d // (2 * LANES), LANES)


def _tile(n, pref):
    t = min(n, pref)
    assert n % t == 0, (n, pref)
    return t


def kernel(x, mem, attn_norm_g, w_in, conv_w, conv_b, conv_ln_g, conv_ln_b, sb_out_g, mem_norm_g, w_mem_kv, q_norm_g, k_norm_g, mem_out_g, w_out, ffn_norm_g, w_peer_q, peer_keys1, peer_keys2, peer_u, peer_v):
    batch, seq, d = x.shape
    depth = w_in.shape[0]
    conv_ch = conv_w.shape[2]
    sb_width = sb_out_g.shape[1]
    mem_width = mem_out_g.shape[1]
    assert d == 8 * LANES and conv_ch % LANES == 0 and sb_width % LANES == 0
    n = batch * seq
    q_off, k_off, v_off = 2 * conv_ch, 2 * conv_ch + sb_width, 2 * conv_ch + 2 * sb_width
    m_off = 2 * conv_ch + 3 * sb_width
    row = lambda v: v.reshape(1, -1)

    x2 = x.reshape(n, d)
    for l in range(depth):
        proj = _in_proj(x2, row(attn_norm_g[l]), w_in[l].astype(BF16), _tile(n, 512))
        conv_o = _conv(proj, conv_w[l], row(conv_b[l]), row(conv_ln_g[l]), row(conv_ln_b[l]),
                       batch, seq, _tile(seq, 512))
        sb = _stickbreak(proj, batch, seq, _tile(seq, 256), q_off // LANES, k_off // LANES,
                         v_off // LANES, sb_width // LANES)
        heads_m = mem_width // HEAD_DIM
        mk, mv = _mem_kv(mem, row(mem_norm_g[l]), w_mem_kv[l].astype(BF16),
                         row(jnp.tile(k_norm_g[l], heads_m)))
        mem_o = _mem_attn(proj, mk, mv, row(jnp.tile(q_norm_g[l], heads_m)), row(mem_out_g[l]),
                          batch, seq, _tile(seq, 512), m_off // mem_width)
        x1, h2, q = _out_proj(conv_o, sb, mem_o, x2, row(sb_out_g[l]), w_out[l].astype(BF16),
                              row(ffn_norm_g[l]), w_peer_q[l].astype(BF16), _tile(n, 512))
        eidx, gates = _peer_topk(q, peer_keys1[l].astype(BF16), peer_keys2[l].astype(BF16),
                                 _tile(n, 256))
        tg = _tile(n, 128)
        eidx_t = eidx.T
        coef = _peer_u(eidx_t, _pack_table(peer_u[l]), h2.reshape(n, 8, LANES), gates, tg)
        x2 = _peer_v(eidx_t, coef, _pack_table(peer_v[l]), x1.reshape(n, 8, LANES), tg).reshape(n, d)
    return x2.reshape(batch, seq, d)
```

```python
import functools

import jax
import jax.numpy as jnp
from jax import lax
from jax.experimental import pallas as pl
from jax.experimental.pallas import tpu as pltpu

F32 = jnp.float32
BF16 = jnp.bfloat16
EPS = 1e-6

HEAD_DIM = 64
LANES = 128
CONV_HALO = 32
SB_ROWS = 128
SB_DEAD = -120.0
PEER_V_GROUP = 4
PEER_U_GROUP = 4
PEER_TOPK = 16
N_KEYS = 128
NEG_INF = float("-inf")
BIG_ID = float(2 ** 24)
VMEM_LIMIT = 56 * 1024 * 1024


def _rms(x, g):
    return x * lax.rsqrt(jnp.mean(x * x, axis=-1, keepdims=True) + EPS) * g


def _dot_nt(a, b):
    return lax.dot_general(a, b, (((1,), (1,)), ((), ())), preferred_element_type=F32)


def _split_bf16(x):
    hi = x.astype(BF16)
    lo = (x - hi.astype(F32)).astype(BF16)
    return hi, lo


def _in_proj_kernel(x_ref, g_ref, w_ref, o_ref):
    h = _rms(x_ref[...], g_ref[...])
    o_ref[...] = jnp.dot(h.astype(BF16), w_ref[...], preferred_element_type=F32).astype(o_ref.dtype)


def _in_proj(x2, g, w, tm):
    n, d = x2.shape
    width = w.shape[1]
    return pl.pallas_call(
        _in_proj_kernel,
        grid=(n // tm,),
        in_specs=[pl.BlockSpec((tm, d), lambda i: (i, 0)),
                  pl.BlockSpec((1, d), lambda i: (0, 0)),
                  pl.BlockSpec((d, width), lambda i: (0, 0))],
        out_specs=pl.BlockSpec((tm, width), lambda i: (i, 0)),
        out_shape=jax.ShapeDtypeStruct((n, width), BF16),
        compiler_params=pltpu.CompilerParams(dimension_semantics=("parallel",),
                                             vmem_limit_bytes=VMEM_LIMIT),
        name="in_proj",
    )(x2, g, w)


def _conv_kernel(a_ref, gate_ref, w_ref, b_ref, lg_ref, lb_ref, o_ref, ubuf, *, tc, width):
    i = pl.program_id(1)

    @pl.when(i == 0)
    def _():
        ubuf[0:CONV_HALO, :] = jnp.zeros((CONV_HALO, ubuf.shape[1]), F32)

    @pl.when(i > 0)
    def _():
        ubuf[0:CONV_HALO, :] = ubuf[tc:tc + CONV_HALO, :]

    a = a_ref[...].astype(F32)
    gate = gate_ref[...].astype(F32)
    ubuf[CONV_HALO:CONV_HALO + tc, :] = a * jax.nn.sigmoid(gate)
    base = CONV_HALO - (width - 1)
    acc = jnp.zeros((tc, ubuf.shape[1]), F32)
    for j in range(width):
        acc = acc + w_ref[j:j + 1, :] * ubuf[base + j:base + j + tc, :]
    y = acc + b_ref[...]
    mu = jnp.mean(y, axis=-1, keepdims=True)
    yc = y - mu
    var = jnp.mean(yc * yc, axis=-1, keepdims=True)
    z = yc * lax.rsqrt(var + EPS) * lg_ref[...] + lb_ref[...]
    o_ref[...] = (z * jax.nn.sigmoid(z)).astype(o_ref.dtype)


def _conv(proj, conv_w, conv_b, ln_g, ln_b, batch, seq, tc):
    n = proj.shape[0]
    width, ch = conv_w.shape
    nt = seq // tc
    kern = functools.partial(_conv_kernel, tc=tc, width=width)
    vec = lambda: pl.BlockSpec((1, ch), lambda b, i: (0, 0))
    return pl.pallas_call(
        kern,
        grid=(batch, nt),
        in_specs=[pl.BlockSpec((tc, ch), lambda b, i: (b * nt + i, 0)),
                  pl.BlockSpec((tc, ch), lambda b, i: (b * nt + i, 1)),
                  pl.BlockSpec((width, ch), lambda b, i: (0, 0)),
                  vec(), vec(), vec()],
        out_specs=pl.BlockSpec((tc, ch), lambda b, i: (b * nt + i, 0)),
        out_shape=jax.ShapeDtypeStruct((n, ch), BF16),
        scratch_shapes=[pltpu.VMEM((tc + CONV_HALO, ch), F32)],
        compiler_params=pltpu.CompilerParams(dimension_semantics=("parallel", "arbitrary"),
                                             vmem_limit_bytes=VMEM_LIMIT),
        name="conv",
    )(proj, proj, conv_w, conv_b, ln_g, ln_b)


def _sb_kernel(q_ref, k_ref, v_ref, o_ref, *scratch, tq):
    n_chain = len(scratch) // 2
    acc_refs, carry_refs = scratch[:n_chain], scratch[n_chain:]
    qi = pl.program_id(2)
    q2 = q_ref[...]
    lane = lax.broadcasted_iota(jnp.int32, (1, LANES), 1)
    row = lax.broadcasted_iota(jnp.int32, (tq, tq), 0)
    col = lax.broadcasted_iota(jnp.int32, (tq, tq), 1)
    suffix = jnp.where(row > col, -1.0, 0.0).astype(BF16)
    scale = jnp.asarray(HEAD_DIM ** -0.5, q2.dtype)
    chains = []
    for h in range(2):
        head = (lane >= h * HEAD_DIM) & (lane < (h + 1) * HEAD_DIM)
        qh = jnp.where(head, q2, jnp.zeros_like(q2)) * scale
        for r0 in range(0, tq, SB_ROWS):
            chains.append((qh[r0:r0 + SB_ROWS], r0))
    for c in range(len(chains)):
        acc_refs[c][...] = jnp.zeros(acc_refs[c].shape, F32)
        carry_refs[c][...] = jnp.zeros(carry_refs[c].shape, F32)

    def stages(kb, diag):
        st = {}
        n = len(chains)

        def qk():
            start = pl.multiple_of(kb * tq, tq)
            st["k"] = k_ref[pl.ds(start, tq), :]
            st["v"] = v_ref[pl.ds(start, tq), :]
            st["z"] = [_dot_nt(qc, st["k"]) for qc, _ in chains]

        def logs():
            st["lg"], st["spb"], st["keep"] = [], [], []
            for z, (_, r0) in zip(st["z"], chains):
                lg = jnp.minimum(z, 0.0) - jnp.log(1.0 + jnp.exp(jnp.minimum(z, -z)))
                sp = z - lg
                if diag:
                    keep = (lax.broadcasted_iota(jnp.int32, (SB_ROWS, tq), 1)
                            < lax.broadcasted_iota(jnp.int32, (SB_ROWS, tq), 0) + r0)
                    sp = jnp.where(keep, sp, 0.0)
                    st["keep"].append(keep)
                st["lg"].append(lg)
                st["spb"].append(sp.astype(BF16))

        def sums():
            st["r"] = [jnp.dot(spb, suffix, preferred_element_type=F32) for spb in st["spb"]]

        def weights():
            st["a"] = []
            for c in range(n):
                after = st["r"][c] + carry_refs[c][...]
                a = jnp.exp(st["lg"][c] + after)
                if diag:
                    a = jnp.where(st["keep"][c], a, 0.0)
                st["a"].append(a.astype(BF16))
                carry_refs[c][...] = after[:, 0:1] - st["spb"][c][:, 0:LANES].astype(F32)[:, 0:1]

        def values():
            for c in range(n):
                acc_refs[c][...] += jnp.dot(st["a"][c], st["v"], preferred_element_type=F32)

        return [qk, logs, sums, weights, values]

    def run(blocks):
        seqs = [stages(kb, diag) for kb, diag in blocks]
        depth = len(seqs[0])
        for step in range(depth + len(seqs) - 1):
            for i, sq in enumerate(seqs):
                if 0 <= step - i < depth:
                    sq[step - i]()

    run([(qi, True)])

    def live(s):
        i, go = s
        return (i < qi // 2) & go

    def pair(s):
        i, _ = s
        run([(qi - 1 - 2 * i, False), (qi - 2 - 2 * i, False)])
        top = functools.reduce(jnp.maximum, [jnp.max(r[...]) for r in carry_refs])
        return i + 1, top > SB_DEAD

    _, go = lax.while_loop(live, pair, (jnp.int32(0), jnp.bool_(True)))

    @pl.when((qi % 2 == 1) & go)
    def _():
        run([(0, False)])

    n_sub = tq // SB_ROWS
    out = [jnp.concatenate([acc_refs[h * n_sub + s][...] for s in range(n_sub)], axis=0) for h in range(2)]
    o_ref[...] = jnp.where(lane < HEAD_DIM, out[0], out[1]).astype(o_ref.dtype)


def _stickbreak(proj, batch, seq, tq, q_col, k_col, v_col, n_pairs):
    n = proj.shape[0]
    nq = seq // tq
    kern = functools.partial(_sb_kernel, tq=tq)
    return pl.pallas_call(
        kern,
        grid=(batch, n_pairs, nq),
        in_specs=[pl.BlockSpec((tq, LANES), lambda b, p, i: (b * nq + i, q_col + p)),
                  pl.BlockSpec((seq, LANES), lambda b, p, i: (b, k_col + p)),
                  pl.BlockSpec((seq, LANES), lambda b, p, i: (b, v_col + p))],
        out_specs=pl.BlockSpec((tq, LANES), lambda b, p, i: (b * nq + i, p)),
        out_shape=jax.ShapeDtypeStruct((n, n_pairs * LANES), F32),
        scratch_shapes=([pltpu.VMEM((SB_ROWS, LANES), F32)] * (2 * tq // SB_ROWS)
                        + [pltpu.VMEM((SB_ROWS, 1), F32)] * (2 * tq // SB_ROWS)),
        compiler_params=pltpu.CompilerParams(
            dimension_semantics=("parallel", "parallel", "arbitrary"),
            vmem_limit_bytes=VMEM_LIMIT),
        name="stickbreak",
    )(proj, proj, proj)


def _head_sumsq(x, blk_ones):
    hi, lo = _split_bf16(x * x)
    return (jnp.dot(hi, blk_ones, preferred_element_type=F32)
            + jnp.dot(lo, blk_ones, preferred_element_type=F32))


def _block_ones(width):
    r = lax.broadcasted_iota(jnp.int32, (width, width), 0) // HEAD_DIM
    c = lax.broadcasted_iota(jnp.int32, (width, width), 1) // HEAD_DIM
    return jnp.where(r == c, 1.0, 0.0).astype(BF16)


def _mem_kv_kernel(mem_ref, g_ref, w_ref, kg_ref, k_ref, v_ref, *, width):
    mn = _rms(mem_ref[...], g_ref[...])
    kv = jnp.dot(mn.astype(BF16), w_ref[...], preferred_element_type=F32)
    k = kv[:, :width]
    ss = _head_sumsq(k, _block_ones(width))
    k_ref[...] = (k * lax.rsqrt(ss * (1.0 / HEAD_DIM) + EPS) * kg_ref[...]).astype(k_ref.dtype)
    v_ref[...] = kv[:, width:].astype(v_ref.dtype)


def _mem_kv(mem, g, w, kg_t):
    b, m, d = mem.shape
    width = w.shape[1] // 2
    kern = functools.partial(_mem_kv_kernel, width=width)
    out = jax.ShapeDtypeStruct((b, m, width), BF16)
    return pl.pallas_call(
        kern,
        grid=(b,),
        in_specs=[pl.BlockSpec((None, m, d), lambda i: (i, 0, 0)),
                  pl.BlockSpec((1, d), lambda i: (0, 0)),
                  pl.BlockSpec((d, 2 * width), lambda i: (0, 0)),
                  pl.BlockSpec((1, width), lambda i: (0, 0))],
        out_specs=[pl.BlockSpec((None, m, width), lambda i: (i, 0, 0)),
                   pl.BlockSpec((None, m, width), lambda i: (i, 0, 0))],
        out_shape=[out, out],
        compiler_params=pltpu.CompilerParams(dimension_semantics=("parallel",),
                                             vmem_limit_bytes=VMEM_LIMIT),
        name="mem_kv",
    )(mem, g, w, kg_t)


def _mem_attn_kernel(q_ref, k_ref, v_ref, qg_ref, og_ref, o_ref, *, width):
    q = q_ref[...].astype(F32)
    ss = _head_sumsq(q, _block_ones(width))
    qn = q * lax.rsqrt(ss * (1.0 / HEAD_DIM) + EPS) * qg_ref[...]
    k = k_ref[...]
    v = v_ref[...]
    lane = lax.broadcasted_iota(jnp.int32, (1, width), 1)
    out = jnp.zeros(q.shape, F32)
    for h in range(width // HEAD_DIM):
        head = (lane >= h * HEAD_DIM) & (lane < (h + 1) * HEAD_DIM)
        qh = jnp.where(head, qn, 0.0).astype(BF16)
        s = _dot_nt(qh, k) * (HEAD_DIM ** -0.5)
        p = jnp.exp(s - jnp.max(s, axis=-1, keepdims=True))
        o = jnp.dot(p.astype(BF16), v, preferred_element_type=F32) / jnp.sum(p, axis=-1, keepdims=True)
        out = jnp.where(head, o, out)
    o_ref[...] = _rms(out, og_ref[...]).astype(o_ref.dtype)


def _mem_attn(proj, k, v, qg_t, og, batch, seq, tm, q_col):
    n = proj.shape[0]
    _, m, width = k.shape
    nt = seq // tm
    kern = functools.partial(_mem_attn_kernel, width=width)
    return pl.pallas_call(
        kern,
        grid=(batch, nt),
        in_specs=[pl.BlockSpec((tm, width), lambda b, i: (b * nt + i, q_col)),
                  pl.BlockSpec((None, m, width), lambda b, i: (b, 0, 0)),
                  pl.BlockSpec((None, m, width), lambda b, i: (b, 0, 0)),
                  pl.BlockSpec((1, width), lambda b, i: (0, 0)),
                  pl.BlockSpec((1, width), lambda b, i: (0, 0))],
        out_specs=pl.BlockSpec((tm, width), lambda b, i: (b * nt + i, 0)),
        out_shape=jax.ShapeDtypeStruct((n, width), BF16),
        compiler_params=pltpu.CompilerParams(dimension_semantics=("parallel", "parallel"),
                                             vmem_limit_bytes=VMEM_LIMIT),
        name="mem_attn",
    )(proj, k, v, qg_t, og)


def _out_proj_kernel(conv_ref, sb_ref, mem_ref, x_ref, sbg_ref, wo_ref, fg_ref, wq_ref,
                     x1_ref, h2_ref, q_ref, *, c_w, s_w):
    sbn = _rms(sb_ref[...], sbg_ref[...]).astype(BF16)
    mixed = jnp.dot(conv_ref[...], wo_ref[0:c_w, :], preferred_element_type=F32)
    mixed = mixed + jnp.dot(sbn, wo_ref[c_w:c_w + s_w, :], preferred_element_type=F32)
    mixed = mixed + jnp.dot(mem_ref[...], wo_ref[c_w + s_w:, :], preferred_element_type=F32)
    x1 = x_ref[...] + mixed
    h2 = _rms(x1, fg_ref[...])
    x1_ref[...] = x1
    h2_ref[...] = h2
    q_ref[...] = jnp.dot(h2.astype(BF16), wq_ref[...], preferred_element_type=F32).astype(q_ref.dtype)


def _out_proj(conv_o, sb, mem_o, x2, sbg, w_out, fg, w_q, tm):
    n, d = x2.shape
    c_w, s_w, m_w = conv_o.shape[1], sb.shape[1], mem_o.shape[1]
    qw = w_q.shape[1]
    kern = functools.partial(_out_proj_kernel, c_w=c_w, s_w=s_w)
    row = lambda w: pl.BlockSpec((tm, w), lambda i: (i, 0))
    full = lambda r, c: pl.BlockSpec((r, c), lambda i: (0, 0))
    return pl.pallas_call(
        kern,
        grid=(n // tm,),
        in_specs=[row(c_w), row(s_w), row(m_w), row(d), full(1, s_w), full(c_w + s_w + m_w, d),
                  full(1, d), full(d, qw)],
        out_specs=[row(d), row(d), row(qw)],
        out_shape=[jax.ShapeDtypeStruct((n, d), F32), jax.ShapeDtypeStruct((n, d), F32),
                   jax.ShapeDtypeStruct((n, qw), BF16)],
        compiler_params=pltpu.CompilerParams(dimension_semantics=("parallel",),
                                             vmem_limit_bytes=VMEM_LIMIT),
        name="out_proj",
    )(conv_o, sb, mem_o, x2, sbg, w_out, fg, w_q)


def _topk_rows(s, ids, k):
    vals, sel = [], []
    for _ in range(k):
        m = jnp.max(s, axis=0, keepdims=True)
        i = jnp.min(jnp.where(s == m, ids, BIG_ID), axis=0, keepdims=True)
        s = jnp.where(ids == i, NEG_INF, s)
        vals.append(m)
        sel.append(i)
    return jnp.concatenate(vals, axis=0), jnp.concatenate(sel, axis=0)


def _peer_topk_kernel(q_ref, k1_ref, k2_ref, eidx_ref, gates_ref, *, half):
    t = q_ref.shape[0]
    key_ids = lax.broadcasted_iota(jnp.int32, (N_KEYS, t), 0).astype(F32)
    q = q_ref[...]
    v1, i1 = _topk_rows(_dot_nt(k1_ref[...], q[:, :half]), key_ids, PEER_TOPK)
    v2, i2 = _topk_rows(_dot_nt(k2_ref[...], q[:, half:]), key_ids, PEER_TOPK)
    cand_s, cand_i = [v1[0:1] + v2], [i1[0:1] * N_KEYS + i2]
    sub = lax.broadcasted_iota(jnp.int32, (8, t), 0)
    for a in range(1, 8):
        ok = sub < (PEER_TOPK // (a + 1))
        cand_s.append(jnp.where(ok, v1[a:a + 1] + v2[0:8], NEG_INF))
        cand_i.append(i1[a:a + 1] * N_KEYS + i2[0:8])
    cand_s.append(v1[8:16] + v2[0:1])
    cand_i.append(i1[8:16] * N_KEYS + i2[0:1])
    top_s, eidx = _topk_rows(jnp.concatenate(cand_s, axis=0), jnp.concatenate(cand_i, axis=0), PEER_TOPK)
    p = jnp.exp(top_s - top_s[0:1])
    eidx_ref[...] = eidx.astype(jnp.int32)
    gates_ref[...] = p / jnp.sum(p, axis=0, keepdims=True)


def _peer_topk(q, k1, k2, tt):
    n, qw = q.shape
    heads, n_keys, half = k1.shape
    kern = functools.partial(_peer_topk_kernel, half=half)
    out_spec = lambda: pl.BlockSpec((PEER_TOPK, tt), lambda i, h: (h, i))
    return pl.pallas_call(
        kern,
        grid=(n // tt, heads),
        in_specs=[pl.BlockSpec((tt, 2 * half), lambda i, h: (i, h)),
                  pl.BlockSpec((None, n_keys, half), lambda i, h: (h, 0, 0)),
                  pl.BlockSpec((None, n_keys, half), lambda i, h: (h, 0, 0))],
        out_specs=[out_spec(), out_spec()],
        out_shape=[jax.ShapeDtypeStruct((heads * PEER_TOPK, n), jnp.int32),
                   jax.ShapeDtypeStruct((heads * PEER_TOPK, n), F32)],
        compiler_params=pltpu.CompilerParams(dimension_semantics=("parallel", "parallel"),
                                             vmem_limit_bytes=VMEM_LIMIT),
        name="peer_topk",
    )(q, k1, k2)


def _unpack_pair(w):
    lo = pltpu.bitcast(w << 16, F32)
    hi = pltpu.bitcast(w & jnp.uint32(0xFFFF0000), F32)
    return lo, hi


def _peer_u_kernel(idx_ref, tbl_ref, h2_ref, gates_ref, coef_ref, act_ref, *y_refs, tg, picks):
    lane = lax.broadcasted_iota(jnp.int32, (1, tg), 1)
    group = len(y_refs)

    def products(t, y_ref):
        hv = h2_ref[t]
        h_lo, h_hi = hv[0:4], hv[4:8]
        for p in range(picks):
            lo, hi = _unpack_pair(tbl_ref[idx_ref[t, p]])
            y_ref[4 * p:4 * p + 4, :] = lo * h_lo + hi * h_hi

    def reduce(t, y_ref):
        part = (y_ref[pl.ds(0, picks, stride=4), :] + y_ref[pl.ds(1, picks, stride=4), :]
                + y_ref[pl.ds(2, picks, stride=4), :] + y_ref[pl.ds(3, picks, stride=4), :])
        act = jnp.sum(part, axis=1, keepdims=True)
        act_ref[...] = jnp.where(lane == t, act, act_ref[...])

    y_refs[-1][...] = jnp.zeros(y_refs[-1].shape, F32)
    act_ref[...] = jnp.zeros(act_ref.shape, F32)

    def body(j, c):
        t0 = group * j
        for k in range(group):
            products(t0 + k, y_refs[k])
            reduce(t0 + k - 1, y_refs[k - 1])
        return c

    lax.fori_loop(0, tg // group, body, 0)
    reduce(tg - 1, y_refs[-1])
    act = act_ref[...]
    gelu = 0.5 * act * (1.0 + lax.erf(act * (2.0 ** -0.5)))
    coef_ref[...] = gates_ref[...] * gelu


def _peer_u(eidx, tbl, h2r, gates, tg):
    n, picks = eidx.shape
    kern = functools.partial(_peer_u_kernel, tg=tg, picks=picks)
    return pl.pallas_call(
        kern,
        grid=(n // tg,),
        in_specs=[pl.BlockSpec((tg, picks), lambda i: (i, 0), memory_space=pltpu.SMEM,
                               pipeline_mode=pl.Buffered(1)),
                  pl.BlockSpec(memory_space=pltpu.VMEM),
                  pl.BlockSpec((tg, 8, LANES), lambda i: (i, 0, 0)),
                  pl.BlockSpec((picks, tg), lambda i: (0, i))],
        out_specs=pl.BlockSpec((picks, tg), lambda i: (0, i)),
        out_shape=jax.ShapeDtypeStruct((picks, n), F32),
        scratch_shapes=([pltpu.VMEM((picks, tg), F32)]
                        + [pltpu.VMEM((4 * picks, LANES), F32)] * PEER_U_GROUP),
        compiler_params=pltpu.CompilerParams(dimension_semantics=("parallel",),
                                             vmem_limit_bytes=VMEM_LIMIT),
        name="peer_u",
    )(eidx, tbl, h2r, gates)


def _peer_v_kernel(idx_ref, coef_ref, tbl_ref, x1_ref, o_ref, chi_ref, clo_ref, *m_refs, tg, picks):
    hi, lo = _split_bf16(coef_ref[...])
    chi_ref[...] = hi
    clo_ref[...] = lo
    tok = lax.broadcasted_iota(jnp.int32, (tg, LANES), 0)

    def spread(t, m_ref):
        onehot = jnp.where(tok == t, 1.0, 0.0).astype(BF16)
        m_ref[...] = (jnp.dot(chi_ref[...], onehot, preferred_element_type=F32)
                      + jnp.dot(clo_ref[...], onehot, preferred_element_type=F32))

    def combine(t, m_ref):
        accs = [jnp.zeros((4, LANES), F32) for _ in range(4)]
        for p in range(picks):
            lo, hi = _unpack_pair(tbl_ref[idx_ref[t, p]])
            w = m_ref[p:p + 1, :]
            s = 2 * (p % 2)
            accs[s] = accs[s] + w * lo
            accs[s + 1] = accs[s + 1] + w * hi
        upd = jnp.concatenate([accs[0] + accs[2], accs[1] + accs[3]], axis=0)
        o_ref[t] = x1_ref[t] + upd

    group = len(m_refs)
    for k, m_ref in enumerate(m_refs):
        spread(k, m_ref)

    def body(j, c):
        t0 = group * j
        for k, m_ref in enumerate(m_refs):
            combine(t0 + k, m_ref)
            spread(t0 + group + k, m_ref)
        return c

    lax.fori_loop(0, tg // group, body, 0)


def _peer_v(eidx, coef, tbl, x1r, tg):
    n, picks = eidx.shape
    kern = functools.partial(_peer_v_kernel, tg=tg, picks=picks)
    return pl.pallas_call(
        kern,
        grid=(n // tg,),
        in_specs=[pl.BlockSpec((tg, picks), lambda i: (i, 0), memory_space=pltpu.SMEM,
                               pipeline_mode=pl.Buffered(1)),
                  pl.BlockSpec((picks, tg), lambda i: (0, i)),
                  pl.BlockSpec(memory_space=pltpu.VMEM),
                  pl.BlockSpec((tg, 8, LANES), lambda i: (i, 0, 0))],
        out_specs=pl.BlockSpec((tg, 8, LANES), lambda i: (i, 0, 0)),
        out_shape=jax.ShapeDtypeStruct(x1r.shape, F32),
        scratch_shapes=([pltpu.VMEM((picks, tg), BF16)] * 2
                        + [pltpu.VMEM((picks, LANES), F32)] * PEER_V_GROUP),
        compiler_params=pltpu.CompilerParams(dimension_semantics=("parallel",),
                                             vmem_limit_bytes=VMEM_LIMIT),
        name="peer_v",
    )(eidx, coef, tbl, x1r)


def _pack_table(t):
    e, d = t.shape
    tb = lax.bitcast_convert_type(t.astype(BF16), jnp.uint16).astype(jnp.uint32)
    packed = tb[:, :d // 2] | (tb[:, d // 2:] << 16)
    return packed.reshape(e, d // (2 * LANES), LANES)


def _tile(n, pref):
    t = min(n, pref)
    assert n % t == 0, (n, pref)
    return t


def kernel(x, mem, attn_norm_g, w_in, conv_w, conv_b, conv_ln_g, conv_ln_b, sb_out_g, mem_norm_g, w_mem_kv, q_norm_g, k_norm_g, mem_out_g, w_out, ffn_norm_g, w_peer_q, peer_keys1, peer_keys2, peer_u, peer_v):
    batch, seq, d = x.shape
    depth = w_in.shape[0]
    conv_ch = conv_w.shape[2]
    sb_width = sb_out_g.shape[1]
    mem_width = mem_out_g.shape[1]
    assert d == 8 * LANES and conv_ch % LANES == 0 and sb_width % LANES == 0
    n = batch * seq
    q_off, k_off, v_off = 2 * conv_ch, 2 * conv_ch + sb_width, 2 * conv_ch + 2 * sb_width
    m_off = 2 * conv_ch + 3 * sb_width
    row = lambda v: v.reshape(1, -1)

    x2 = x.reshape(n, d)
    for l in range(depth):
        proj = _in_proj(x2, row(attn_norm_g[l]), w_in[l].astype(BF16), _tile(n, 512))
        conv_o = _conv(proj, conv_w[l], row(conv_b[l]), row(conv_ln_g[l]), row(conv_ln_b[l]),
                       batch, seq, _tile(seq, 512))
        sb = _stickbreak(proj, batch, seq, _tile(seq, 256), q_off // LANES, k_off // LANES,
                         v_off // LANES, sb_width // LANES)
        heads_m = mem_width // HEAD_DIM
        mk, mv = _mem_kv(mem, row(mem_norm_g[l]), w_mem_kv[l].astype(BF16),
                         row(jnp.tile(k_norm_g[l], heads_m)))
        mem_o = _mem_attn(proj, mk, mv, row(jnp.tile(q_norm_g[l], heads_m)), row(mem_out_g[l]),
                          batch, seq, _tile(seq, 512), m_off // mem_width)
        x1, h2, q = _out_proj(conv_o, sb, mem_o, x2, row(sb_out_g[l]), w_out[l].astype(BF16),
                              row(ffn_norm_g[l]), w_peer_q[l].astype(BF16), _tile(n, 512))
        eidx, gates = _peer_topk(q, peer_keys1[l].astype(BF16), peer_keys2[l].astype(BF16),
                                 _tile(n, 256))
        tg = _tile(n, 128)
        eidx_t = eidx.T
        coef = _peer_u(eidx_t, _pack_table(peer_u[l]), h2.reshape(n, 8, LANES), gates, tg)
        x2 = _peer_v(eidx_t, coef, _pack_table(peer_v[l]), x1.reshape(n, 8, LANES), tg).reshape(n, d)
    return x2.reshape(batch, seq, d)
```

```python
import functools

import jax
import jax.numpy as jnp
from jax import lax
from jax.experimental import pallas as pl
from jax.experimental.pallas import tpu as pltpu
from jax.experimental.pallas import tpu_sc as plsc

F32 = jnp.float32
BF16 = jnp.bfloat16
EPS = 1e-6

HEAD_DIM = 64
LANES = 128
CONV_HALO = 32
SB_ROWS = 128
SB_DEAD = -120.0
SC_CORES, SC_SUBCORES = 2, 16
SC_ROWS = 64
SC_IDX = 4096
PEER_V_GROUP = 4
PEER_U_GROUP = 8
PEER_TOPK = 16
N_KEYS = 128
NEG_INF = float("-inf")
BIG_ID = float(2 ** 24)
VMEM_LIMIT = 56 * 1024 * 1024


def _rms(x, g):
    return x * lax.rsqrt(jnp.mean(x * x, axis=-1, keepdims=True) + EPS) * g


def _dot_nt(a, b):
    return lax.dot_general(a, b, (((1,), (1,)), ((), ())), preferred_element_type=F32)


def _split_bf16(x):
    hi = x.astype(BF16)
    lo = (x - hi.astype(F32)).astype(BF16)
    return hi, lo


def _in_proj_kernel(x_ref, g_ref, w_ref, o_ref):
    h = _rms(x_ref[...], g_ref[...])
    o_ref[...] = jnp.dot(h.astype(BF16), w_ref[...], preferred_element_type=F32).astype(o_ref.dtype)


def _in_proj(x2, g, w, tm):
    n, d = x2.shape
    width = w.shape[1]
    return pl.pallas_call(
        _in_proj_kernel,
        grid=(n // tm,),
        in_specs=[pl.BlockSpec((tm, d), lambda i: (i, 0)),
                  pl.BlockSpec((1, d), lambda i: (0, 0)),
                  pl.BlockSpec((d, width), lambda i: (0, 0))],
        out_specs=pl.BlockSpec((tm, width), lambda i: (i, 0)),
        out_shape=jax.ShapeDtypeStruct((n, width), BF16),
        compiler_params=pltpu.CompilerParams(dimension_semantics=("parallel",),
                                             vmem_limit_bytes=VMEM_LIMIT),
        name="in_proj",
    )(x2, g, w)


def _conv_kernel(a_ref, gate_ref, w_ref, b_ref, lg_ref, lb_ref, o_ref, ubuf, *, tc, width):
    i = pl.program_id(1)

    @pl.when(i == 0)
    def _():
        ubuf[0:CONV_HALO, :] = jnp.zeros((CONV_HALO, ubuf.shape[1]), F32)

    @pl.when(i > 0)
    def _():
        ubuf[0:CONV_HALO, :] = ubuf[tc:tc + CONV_HALO, :]

    a = a_ref[...].astype(F32)
    gate = gate_ref[...].astype(F32)
    ubuf[CONV_HALO:CONV_HALO + tc, :] = a * jax.nn.sigmoid(gate)
    base = CONV_HALO - (width - 1)
    acc = jnp.zeros((tc, ubuf.shape[1]), F32)
    for j in range(width):
        acc = acc + w_ref[j:j + 1, :] * ubuf[base + j:base + j + tc, :]
    y = acc + b_ref[...]
    mu = jnp.mean(y, axis=-1, keepdims=True)
    yc = y - mu
    var = jnp.mean(yc * yc, axis=-1, keepdims=True)
    z = yc * lax.rsqrt(var + EPS) * lg_ref[...] + lb_ref[...]
    o_ref[...] = (z * jax.nn.sigmoid(z)).astype(o_ref.dtype)


def _conv(proj, conv_w, conv_b, ln_g, ln_b, batch, seq, tc):
    n = proj.shape[0]
    width, ch = conv_w.shape
    nt = seq // tc
    kern = functools.partial(_conv_kernel, tc=tc, width=width)
    vec = lambda: pl.BlockSpec((1, ch), lambda b, i: (0, 0))
    return pl.pallas_call(
        kern,
        grid=(batch, nt),
        in_specs=[pl.BlockSpec((tc, ch), lambda b, i: (b * nt + i, 0)),
                  pl.BlockSpec((tc, ch), lambda b, i: (b * nt + i, 1)),
                  pl.BlockSpec((width, ch), lambda b, i: (0, 0)),
                  vec(), vec(), vec()],
        out_specs=pl.BlockSpec((tc, ch), lambda b, i: (b * nt + i, 0)),
        out_shape=jax.ShapeDtypeStruct((n, ch), BF16),
        scratch_shapes=[pltpu.VMEM((tc + CONV_HALO, ch), F32)],
        compiler_params=pltpu.CompilerParams(dimension_semantics=("parallel", "arbitrary"),
                                             vmem_limit_bytes=VMEM_LIMIT),
        name="conv",
    )(proj, proj, conv_w, conv_b, ln_g, ln_b)


def _sb_kernel(q_ref, k_ref, v_ref, o_ref, *scratch, tq):
    n_chain = len(scratch) // 2
    acc_refs, carry_refs = scratch[:n_chain], scratch[n_chain:]
    qi = pl.program_id(2)
    q2 = q_ref[...]
    lane = lax.broadcasted_iota(jnp.int32, (1, LANES), 1)
    row = lax.broadcasted_iota(jnp.int32, (tq, tq), 0)
    col = lax.broadcasted_iota(jnp.int32, (tq, tq), 1)
    suffix = jnp.where(row > col, -1.0, 0.0).astype(BF16)
    scale = jnp.asarray(HEAD_DIM ** -0.5, q2.dtype)
    chains = []
    for h in range(2):
        head = (lane >= h * HEAD_DIM) & (lane < (h + 1) * HEAD_DIM)
        qh = jnp.where(head, q2, jnp.zeros_like(q2)) * scale
        for r0 in range(0, tq, SB_ROWS):
            chains.append((qh[r0:r0 + SB_ROWS], r0))
    for c in range(len(chains)):
        acc_refs[c][...] = jnp.zeros(acc_refs[c].shape, F32)
        carry_refs[c][...] = jnp.zeros(carry_refs[c].shape, F32)

    def stages(kb, diag):
        st = {}
        n = len(chains)

        def qk():
            start = pl.multiple_of(kb * tq, tq)
            st["k"] = k_ref[pl.ds(start, tq), :]
            st["v"] = v_ref[pl.ds(start, tq), :]
            st["z"] = [_dot_nt(qc, st["k"]) for qc, _ in chains]

        def logs():
            st["lg"], st["spb"], st["keep"] = [], [], []
            for z, (_, r0) in zip(st["z"], chains):
                lg = jnp.minimum(z, 0.0) - jnp.log(1.0 + jnp.exp(jnp.minimum(z, -z)))
                sp = z - lg
                if diag:
                    keep = (lax.broadcasted_iota(jnp.int32, (SB_ROWS, tq), 1)
                            < lax.broadcasted_iota(jnp.int32, (SB_ROWS, tq), 0) + r0)
                    sp = jnp.where(keep, sp, 0.0)
                    st["keep"].append(keep)
                st["lg"].append(lg)
                st["spb"].append(sp.astype(BF16))

        def sums():
            st["r"] = [jnp.dot(spb, suffix, preferred_element_type=F32) for spb in st["spb"]]

        def weights():
            st["a"] = []
            for c in range(n):
                after = st["r"][c] + carry_refs[c][...]
                a = jnp.exp(st["lg"][c] + after)
                if diag:
                    a = jnp.where(st["keep"][c], a, 0.0)
                st["a"].append(a.astype(BF16))
                carry_refs[c][...] = after[:, 0:1] - st["spb"][c][:, 0:LANES].astype(F32)[:, 0:1]

        def values():
            for c in range(n):
                acc_refs[c][...] += jnp.dot(st["a"][c], st["v"], preferred_element_type=F32)

        return [qk, logs, sums, weights, values]

    def run(blocks):
        seqs = [stages(kb, diag) for kb, diag in blocks]
        depth = len(seqs[0])
        for step in range(depth + len(seqs) - 1):
            for i, sq in enumerate(seqs):
                if 0 <= step - i < depth:
                    sq[step - i]()

    run([(qi, True)])

    def live(s):
        i, go = s
        return (i < qi // 2) & go

    def pair(s):
        i, _ = s
        run([(qi - 1 - 2 * i, False), (qi - 2 - 2 * i, False)])
        top = functools.reduce(jnp.maximum, [jnp.max(r[...]) for r in carry_refs])
        return i + 1, top > SB_DEAD

    _, go = lax.while_loop(live, pair, (jnp.int32(0), jnp.bool_(True)))

    @pl.when((qi % 2 == 1) & go)
    def _():
        run([(0, False)])

    n_sub = tq // SB_ROWS
    out = [jnp.concatenate([acc_refs[h * n_sub + s][...] for s in range(n_sub)], axis=0) for h in range(2)]
    o_ref[...] = jnp.where(lane < HEAD_DIM, out[0], out[1]).astype(o_ref.dtype)


def _stickbreak(proj, batch, seq, tq, q_col, k_col, v_col, n_pairs):
    n = proj.shape[0]
    nq = seq // tq
    kern = functools.partial(_sb_kernel, tq=tq)
    return pl.pallas_call(
        kern,
        grid=(batch, n_pairs, nq),
        in_specs=[pl.BlockSpec((tq, LANES), lambda b, p, i: (b * nq + i, q_col + p)),
                  pl.BlockSpec((seq, LANES), lambda b, p, i: (b, k_col + p)),
                  pl.BlockSpec((seq, LANES), lambda b, p, i: (b, v_col + p))],
        out_specs=pl.BlockSpec((tq, LANES), lambda b, p, i: (b * nq + i, p)),
        out_shape=jax.ShapeDtypeStruct((n, n_pairs * LANES), F32),
        scratch_shapes=([pltpu.VMEM((SB_ROWS, LANES), F32)] * (2 * tq // SB_ROWS)
                        + [pltpu.VMEM((SB_ROWS, 1), F32)] * (2 * tq // SB_ROWS)),
        compiler_params=pltpu.CompilerParams(
            dimension_semantics=("parallel", "parallel", "arbitrary"),
            vmem_limit_bytes=VMEM_LIMIT),
        name="stickbreak",
    )(proj, proj, proj)


def _head_sumsq(x, blk_ones):
    hi, lo = _split_bf16(x * x)
    return (jnp.dot(hi, blk_ones, preferred_element_type=F32)
            + jnp.dot(lo, blk_ones, preferred_element_type=F32))


def _block_ones(width):
    r = lax.broadcasted_iota(jnp.int32, (width, width), 0) // HEAD_DIM
    c = lax.broadcasted_iota(jnp.int32, (width, width), 1) // HEAD_DIM
    return jnp.where(r == c, 1.0, 0.0).astype(BF16)


def _mem_kv_kernel(mem_ref, g_ref, w_ref, kg_ref, k_ref, v_ref, *, width):
    mn = _rms(mem_ref[...], g_ref[...])
    kv = jnp.dot(mn.astype(BF16), w_ref[...], preferred_element_type=F32)
    k = kv[:, :width]
    ss = _head_sumsq(k, _block_ones(width))
    k_ref[...] = (k * lax.rsqrt(ss * (1.0 / HEAD_DIM) + EPS) * kg_ref[...]).astype(k_ref.dtype)
    v_ref[...] = kv[:, width:].astype(v_ref.dtype)


def _mem_kv(mem, g, w, kg_t):
    b, m, d = mem.shape
    width = w.shape[1] // 2
    kern = functools.partial(_mem_kv_kernel, width=width)
    out = jax.ShapeDtypeStruct((b, m, width), BF16)
    return pl.pallas_call(
        kern,
        grid=(b,),
        in_specs=[pl.BlockSpec((None, m, d), lambda i: (i, 0, 0)),
                  pl.BlockSpec((1, d), lambda i: (0, 0)),
                  pl.BlockSpec((d, 2 * width), lambda i: (0, 0)),
                  pl.BlockSpec((1, width), lambda i: (0, 0))],
        out_specs=[pl.BlockSpec((None, m, width), lambda i: (i, 0, 0)),
                   pl.BlockSpec((None, m, width), lambda i: (i, 0, 0))],
        out_shape=[out, out],
        compiler_params=pltpu.CompilerParams(dimension_semantics=("parallel",),
                                             vmem_limit_bytes=VMEM_LIMIT),
        name="mem_kv",
    )(mem, g, w, kg_t)


def _mem_attn_kernel(q_ref, k_ref, v_ref, qg_ref, og_ref, o_ref, *, width):
    q = q_ref[...].astype(F32)
    ss = _head_sumsq(q, _block_ones(width))
    qn = q * lax.rsqrt(ss * (1.0 / HEAD_DIM) + EPS) * qg_ref[...]
    k = k_ref[...]
    v = v_ref[...]
    lane = lax.broadcasted_iota(jnp.int32, (1, width), 1)
    out = jnp.zeros(q.shape, F32)
    for h in range(width // HEAD_DIM):
        head = (lane >= h * HEAD_DIM) & (lane < (h + 1) * HEAD_DIM)
        qh = jnp.where(head, qn, 0.0).astype(BF16)
        s = _dot_nt(qh, k) * (HEAD_DIM ** -0.5)
        p = jnp.exp(s - jnp.max(s, axis=-1, keepdims=True))
        o = jnp.dot(p.astype(BF16), v, preferred_element_type=F32) / jnp.sum(p, axis=-1, keepdims=True)
        out = jnp.where(head, o, out)
    o_ref[...] = _rms(out, og_ref[...]).astype(o_ref.dtype)


def _mem_attn(proj, k, v, qg_t, og, batch, seq, tm, q_col):
    n = proj.shape[0]
    _, m, width = k.shape
    nt = seq // tm
    kern = functools.partial(_mem_attn_kernel, width=width)
    return pl.pallas_call(
        kern,
        grid=(batch, nt),
        in_specs=[pl.BlockSpec((tm, width), lambda b, i: (b * nt + i, q_col)),
                  pl.BlockSpec((None, m, width), lambda b, i: (b, 0, 0)),
                  pl.BlockSpec((None, m, width), lambda b, i: (b, 0, 0)),
                  pl.BlockSpec((1, width), lambda b, i: (0, 0)),
                  pl.BlockSpec((1, width), lambda b, i: (0, 0))],
        out_specs=pl.BlockSpec((tm, width), lambda b, i: (b * nt + i, 0)),
        out_shape=jax.ShapeDtypeStruct((n, width), BF16),
        compiler_params=pltpu.CompilerParams(dimension_semantics=("parallel", "parallel"),
                                             vmem_limit_bytes=VMEM_LIMIT),
        name="mem_attn",
    )(proj, k, v, qg_t, og)


def _out_proj_kernel(conv_ref, sb_ref, mem_ref, x_ref, sbg_ref, wo_ref, fg_ref, wq_ref,
                     x1_ref, h2_ref, q_ref, *, c_w, s_w):
    sbn = _rms(sb_ref[...], sbg_ref[...]).astype(BF16)
    mixed = jnp.dot(conv_ref[...], wo_ref[0:c_w, :], preferred_element_type=F32)
    mixed = mixed + jnp.dot(sbn, wo_ref[c_w:c_w + s_w, :], preferred_element_type=F32)
    mixed = mixed + jnp.dot(mem_ref[...], wo_ref[c_w + s_w:, :], preferred_element_type=F32)
    x1 = x_ref[...] + mixed
    h2 = _rms(x1, fg_ref[...])
    x1_ref[...] = x1
    h2_ref[...] = h2
    q_ref[...] = jnp.dot(h2.astype(BF16), wq_ref[...], preferred_element_type=F32).astype(q_ref.dtype)


def _out_proj(conv_o, sb, mem_o, x2, sbg, w_out, fg, w_q, tm):
    n, d = x2.shape
    c_w, s_w, m_w = conv_o.shape[1], sb.shape[1], mem_o.shape[1]
    qw = w_q.shape[1]
    kern = functools.partial(_out_proj_kernel, c_w=c_w, s_w=s_w)
    row = lambda w: pl.BlockSpec((tm, w), lambda i: (i, 0))
    full = lambda r, c: pl.BlockSpec((r, c), lambda i: (0, 0))
    return pl.pallas_call(
        kern,
        grid=(n // tm,),
        in_specs=[row(c_w), row(s_w), row(m_w), row(d), full(1, s_w), full(c_w + s_w + m_w, d),
                  full(1, d), full(d, qw)],
        out_specs=[row(d), row(d), row(qw)],
        out_shape=[jax.ShapeDtypeStruct((n, d), F32), jax.ShapeDtypeStruct((n, d), F32),
                   jax.ShapeDtypeStruct((n, qw), BF16)],
        compiler_params=pltpu.CompilerParams(dimension_semantics=("parallel",),
                                             vmem_limit_bytes=VMEM_LIMIT),
        name="out_proj",
    )(conv_o, sb, mem_o, x2, sbg, w_out, fg, w_q)


def _topk_rows(s, ids, k):
    vals, sel = [], []
    for _ in range(k):
        m = jnp.max(s, axis=0, keepdims=True)
        i = jnp.min(jnp.where(s == m, ids, BIG_ID), axis=0, keepdims=True)
        s = jnp.where(ids == i, NEG_INF, s)
        vals.append(m)
        sel.append(i)
    return jnp.concatenate(vals, axis=0), jnp.concatenate(sel, axis=0)


def _peer_topk_kernel(q_ref, k1_ref, k2_ref, eidx_ref, gates_ref, *, half):
    t = q_ref.shape[0]
    key_ids = lax.broadcasted_iota(jnp.int32, (N_KEYS, t), 0).astype(F32)
    q = q_ref[...]
    v1, i1 = _topk_rows(_dot_nt(k1_ref[...], q[:, :half]), key_ids, PEER_TOPK)
    v2, i2 = _topk_rows(_dot_nt(k2_ref[...], q[:, half:]), key_ids, PEER_TOPK)
    cand_s, cand_i = [v1[0:1] + v2], [i1[0:1] * N_KEYS + i2]
    sub = lax.broadcasted_iota(jnp.int32, (8, t), 0)
    for a in range(1, 8):
        ok = sub < (PEER_TOPK // (a + 1))
        cand_s.append(jnp.where(ok, v1[a:a + 1] + v2[0:8], NEG_INF))
        cand_i.append(i1[a:a + 1] * N_KEYS + i2[0:8])
    cand_s.append(v1[8:16] + v2[0:1])
    cand_i.append(i1[8:16] * N_KEYS + i2[0:1])
    top_s, eidx = _topk_rows(jnp.concatenate(cand_s, axis=0), jnp.concatenate(cand_i, axis=0), PEER_TOPK)
    p = jnp.exp(top_s - top_s[0:1])
    eidx_ref[...] = eidx.astype(jnp.int32)
    gates_ref[...] = p / jnp.sum(p, axis=0, keepdims=True)


def _peer_topk(q, k1, k2, tt):
    n, qw = q.shape
    heads, n_keys, half = k1.shape
    kern = functools.partial(_peer_topk_kernel, half=half)
    out_spec = lambda: pl.BlockSpec((PEER_TOPK, tt), lambda i, h: (h, i))
    return pl.pallas_call(
        kern,
        grid=(n // tt, heads),
        in_specs=[pl.BlockSpec((tt, 2 * half), lambda i, h: (i, h)),
                  pl.BlockSpec((None, n_keys, half), lambda i, h: (h, 0, 0)),
                  pl.BlockSpec((None, n_keys, half), lambda i, h: (h, 0, 0))],
        out_specs=[out_spec(), out_spec()],
        out_shape=[jax.ShapeDtypeStruct((heads * PEER_TOPK, n), jnp.int32),
                   jax.ShapeDtypeStruct((heads * PEER_TOPK, n), F32)],
        compiler_params=pltpu.CompilerParams(dimension_semantics=("parallel", "parallel"),
                                             vmem_limit_bytes=VMEM_LIMIT),
        name="peer_topk",
    )(q, k1, k2)


def _unpack_pair(w):
    lo = pltpu.bitcast(w << 16, F32)
    hi = pltpu.bitcast(w & jnp.uint32(0xFFFF0000), F32)
    return lo, hi


def _peer_u_kernel(idx_ref, tbl_ref, h2_ref, gates_ref, coef_ref, act_ref, *y_refs, tg, picks):
    lane = lax.broadcasted_iota(jnp.int32, (1, tg), 1)
    group = len(y_refs)

    def products(t, y_ref):
        hv = h2_ref[t]
        h_lo, h_hi = hv[0:4], hv[4:8]
        for p in range(picks):
            lo, hi = _unpack_pair(tbl_ref[idx_ref[t, p]])
            y_ref[4 * p:4 * p + 4, :] = lo * h_lo + hi * h_hi

    def reduce(t, y_ref):
        part = (y_ref[pl.ds(0, picks, stride=4), :] + y_ref[pl.ds(1, picks, stride=4), :]
                + y_ref[pl.ds(2, picks, stride=4), :] + y_ref[pl.ds(3, picks, stride=4), :])
        act = jnp.sum(part, axis=1, keepdims=True)
        act_ref[...] = jnp.where(lane == t, act, act_ref[...])

    y_refs[-1][...] = jnp.zeros(y_refs[-1].shape, F32)
    act_ref[...] = jnp.zeros(act_ref.shape, F32)

    def body(j, c):
        t0 = group * j
        for k in range(group):
            products(t0 + k, y_refs[k])
            reduce(t0 + k - 1, y_refs[k - 1])
        return c

    lax.fori_loop(0, tg // group, body, 0)
    reduce(tg - 1, y_refs[-1])
    act = act_ref[...]
    gelu = 0.5 * act * (1.0 + lax.erf(act * (2.0 ** -0.5)))
    coef_ref[...] = gates_ref[...] * gelu


def _peer_u(eidx, tbl, h2r, gates, tg):
    n, picks = eidx.shape
    kern = functools.partial(_peer_u_kernel, tg=tg, picks=picks)
    return pl.pallas_call(
        kern,
        grid=(n // tg,),
        in_specs=[pl.BlockSpec((tg, picks), lambda i: (i, 0), memory_space=pltpu.SMEM,
                               pipeline_mode=pl.Buffered(1)),
                  pl.BlockSpec(memory_space=pltpu.VMEM),
                  pl.BlockSpec((tg, 8, LANES), lambda i: (i, 0, 0)),
                  pl.BlockSpec((picks, tg), lambda i: (0, i))],
        out_specs=pl.BlockSpec((picks, tg), lambda i: (0, i)),
        out_shape=jax.ShapeDtypeStruct((picks, n), F32),
        scratch_shapes=([pltpu.VMEM((picks, tg), F32)]
                        + [pltpu.VMEM((4 * picks, LANES), F32)] * PEER_U_GROUP),
        compiler_params=pltpu.CompilerParams(dimension_semantics=("parallel",),
                                             vmem_limit_bytes=VMEM_LIMIT),
        name="peer_u",
    )(eidx, tbl, h2r, gates)


def _peer_v_kernel(idx_ref, coef_ref, tbl_ref, x1_ref, o_ref, chi_ref, clo_ref, *m_refs, tg, picks):
    hi, lo = _split_bf16(coef_ref[...])
    chi_ref[...] = hi
    clo_ref[...] = lo
    tok = lax.broadcasted_iota(jnp.int32, (tg, LANES), 0)

    def spread(t, m_ref):
        onehot = jnp.where(tok == t, 1.0, 0.0).astype(BF16)
        m_ref[...] = (jnp.dot(chi_ref[...], onehot, preferred_element_type=F32)
                      + jnp.dot(clo_ref[...], onehot, preferred_element_type=F32))

    def combine(t, m_ref):
        accs = [jnp.zeros((4, LANES), F32) for _ in range(4)]
        for p in range(picks):
            lo, hi = _unpack_pair(tbl_ref[idx_ref[t, p]])
            w = m_ref[p:p + 1, :]
            s = 2 * (p % 2)
            accs[s] = accs[s] + w * lo
            accs[s + 1] = accs[s + 1] + w * hi
        upd = jnp.concatenate([accs[0] + accs[2], accs[1] + accs[3]], axis=0)
        o_ref[t] = x1_ref[t] + upd

    group = len(m_refs)
    for k, m_ref in enumerate(m_refs):
        spread(k, m_ref)

    def body(j, c):
        t0 = group * j
        for k, m_ref in enumerate(m_refs):
            combine(t0 + k, m_ref)
            spread(t0 + group + k, m_ref)
        return c

    lax.fori_loop(0, tg // group, body, 0)


def _peer_v(eidx, coef, tbl, x1r, tg):
    n, picks = eidx.shape
    kern = functools.partial(_peer_v_kernel, tg=tg, picks=picks)
    return pl.pallas_call(
        kern,
        grid=(n // tg,),
        in_specs=[pl.BlockSpec((tg, picks), lambda i: (i, 0), memory_space=pltpu.SMEM,
                               pipeline_mode=pl.Buffered(1)),
                  pl.BlockSpec((picks, tg), lambda i: (0, i)),
                  pl.BlockSpec(memory_space=pltpu.VMEM),
                  pl.BlockSpec((tg, 8, LANES), lambda i: (i, 0, 0))],
        out_specs=pl.BlockSpec((tg, 8, LANES), lambda i: (i, 0, 0)),
        out_shape=jax.ShapeDtypeStruct(x1r.shape, F32),
        scratch_shapes=([pltpu.VMEM((picks, tg), BF16)] * 2
                        + [pltpu.VMEM((picks, LANES), F32)] * PEER_V_GROUP),
        compiler_params=pltpu.CompilerParams(dimension_semantics=("parallel",),
                                             vmem_limit_bytes=VMEM_LIMIT),
        name="peer_v",
    )(eidx, coef, tbl, x1r)


def _sc_gather_rows(table, idx):
    b, w = idx.shape[0], table.shape[1]
    workers = SC_CORES * SC_SUBCORES
    per_w = b // workers
    step = 2 * SC_ROWS
    assert b % workers == 0 and per_w % SC_IDX == 0 and SC_IDX % step == 0
    mesh = plsc.VectorSubcoreMesh(core_axis_name="c", subcore_axis_name="s")

    def body(table_hbm, idx_hbm, out_hbm, idx_v, rows0, rows1, g0, g1, w0, w1):
        wid = lax.axis_index("s") * SC_CORES + lax.axis_index("c")
        base = wid * per_w

        @pl.loop(0, per_w // SC_IDX)
        def _(o):
            off = pl.multiple_of(base + o * SC_IDX, SC_IDX)
            pltpu.sync_copy(idx_hbm.at[pl.ds(off, SC_IDX)], idx_v)

            @pl.loop(0, SC_IDX // step)
            def _(j):
                r0 = pl.multiple_of(j * step, step)
                r1 = pl.multiple_of(r0 + SC_ROWS, SC_ROWS)
                ga = pltpu.async_copy(table_hbm.at[idx_v.at[pl.ds(r0, SC_ROWS)]], rows0, g0)
                gb = pltpu.async_copy(table_hbm.at[idx_v.at[pl.ds(r1, SC_ROWS)]], rows1, g1)
                ga.wait()
                wa = pltpu.async_copy(rows0, out_hbm.at[pl.ds(pl.multiple_of(off + r0, SC_ROWS), SC_ROWS)], w0)
                gb.wait()
                wb = pltpu.async_copy(rows1, out_hbm.at[pl.ds(pl.multiple_of(off + r1, SC_ROWS), SC_ROWS)], w1)
                wa.wait()
                wb.wait()

    return pl.kernel(
        body,
        out_type=jax.ShapeDtypeStruct((b, w), table.dtype),
        mesh=mesh,
        scratch_types=[pltpu.VMEM((SC_IDX,), jnp.int32), pltpu.VMEM((SC_ROWS, w), table.dtype),
                       pltpu.VMEM((SC_ROWS, w), table.dtype)] + [pltpu.SemaphoreType.DMA] * 4,
        name="sc_gather_rows",
    )(table, idx)


def _peer_v_dense_kernel(rows_ref, coef_ref, x1_ref, o_ref, *, tt, picks):
    coef = coef_ref[...].T
    for t in range(tt):
        lo, hi = _unpack_pair(rows_ref[t * picks:(t + 1) * picks, :])
        c = coef[:, t:t + 1]
        upd = jnp.concatenate([jnp.sum(c * lo, axis=0, keepdims=True),
                               jnp.sum(c * hi, axis=0, keepdims=True)], axis=1)
        o_ref[t:t + 1, :] = x1_ref[t:t + 1, :] + upd


def _peer_v_dense(rows, coef_t, x1, tt):
    n, picks = coef_t.shape
    d = x1.shape[1]
    kern = functools.partial(_peer_v_dense_kernel, tt=tt, picks=picks)
    return pl.pallas_call(
        kern,
        grid=(n // tt,),
        in_specs=[pl.BlockSpec((tt * picks, d // 2), lambda i: (i, 0)),
                  pl.BlockSpec((tt, picks), lambda i: (i, 0)),
                  pl.BlockSpec((tt, d), lambda i: (i, 0))],
        out_specs=pl.BlockSpec((tt, d), lambda i: (i, 0)),
        out_shape=jax.ShapeDtypeStruct((n, d), F32),
        compiler_params=pltpu.CompilerParams(dimension_semantics=("parallel",),
                                             vmem_limit_bytes=VMEM_LIMIT),
        name="peer_v_dense",
    )(rows, coef_t, x1)


def _pack_table(t):
    e, d = t.shape
    tb = lax.bitcast_convert_type(t.astype(BF16), jnp.uint16).astype(jnp.uint32)
    packed = tb[:, :d // 2] | (tb[:, d // 2:] << 16)
    return packed.reshape(e, d // (2 * LANES), LANES)


def _tile(n, pref):
    t = min(n, pref)
    assert n % t == 0, (n, pref)
    return t


def kernel(x, mem, attn_norm_g, w_in, conv_w, conv_b, conv_ln_g, conv_ln_b, sb_out_g, mem_norm_g, w_mem_kv, q_norm_g, k_norm_g, mem_out_g, w_out, ffn_norm_g, w_peer_q, peer_keys1, peer_keys2, peer_u, peer_v):
    batch, seq, d = x.shape
    depth = w_in.shape[0]
    conv_ch = conv_w.shape[2]
    sb_width = sb_out_g.shape[1]
    mem_width = mem_out_g.shape[1]
    assert d == 8 * LANES and conv_ch % LANES == 0 and sb_width % LANES == 0
    n = batch * seq
    q_off, k_off, v_off = 2 * conv_ch, 2 * conv_ch + sb_width, 2 * conv_ch + 2 * sb_width
    m_off = 2 * conv_ch + 3 * sb_width
    row = lambda v: v.reshape(1, -1)

    x2 = x.reshape(n, d)
    for l in range(depth):
        proj = _in_proj(x2, row(attn_norm_g[l]), w_in[l].astype(BF16), _tile(n, 512))
        conv_o = _conv(proj, conv_w[l], row(conv_b[l]), row(conv_ln_g[l]), row(conv_ln_b[l]),
                       batch, seq, _tile(seq, 512))
        sb = _stickbreak(proj, batch, seq, _tile(seq, 256), q_off // LANES, k_off // LANES,
                         v_off // LANES, sb_width // LANES)
        heads_m = mem_width // HEAD_DIM
        mk, mv = _mem_kv(mem, row(mem_norm_g[l]), w_mem_kv[l].astype(BF16),
                         row(jnp.tile(k_norm_g[l], heads_m)))
        mem_o = _mem_attn(proj, mk, mv, row(jnp.tile(q_norm_g[l], heads_m)), row(mem_out_g[l]),
                          batch, seq, _tile(seq, 512), m_off // mem_width)
        x1, h2, q = _out_proj(conv_o, sb, mem_o, x2, row(sb_out_g[l]), w_out[l].astype(BF16),
                              row(ffn_norm_g[l]), w_peer_q[l].astype(BF16), _tile(n, 512))
        eidx, gates = _peer_topk(q, peer_keys1[l].astype(BF16), peer_keys2[l].astype(BF16),
                                 _tile(n, 256))
        tg = _tile(n, 128)
        eidx_t = eidx.T
        v_rows = _sc_gather_rows(_pack_table(peer_v[l]).reshape(peer_v.shape[1], d // 2),
                                 eidx_t.reshape(-1))
        coef = _peer_u(eidx_t, _pack_table(peer_u[l]), h2.reshape(n, 8, LANES), gates, tg)
        x2 = _peer_v_dense(v_rows, coef.T, x1, _tile(n, 32))
    return x2.reshape(batch, seq, d)
```

```python
import functools

import jax
import jax.numpy as jnp
from jax import lax
from jax.experimental import pallas as pl
from jax.experimental.pallas import tpu as pltpu
from jax.experimental.pallas import tpu_sc as plsc

F32 = jnp.float32
BF16 = jnp.bfloat16
EPS = 1e-6

HEAD_DIM = 64
LANES = 128
CONV_HALO = 32
SB_ROWS = 128
SB_DEAD = -120.0
SC_CORES, SC_SUBCORES = 2, 16
SC_ROWS = 64
SC_IDX = 4096
PEER_U_GROUP = 8
PEER_TOPK = 16
N_KEYS = 128
NEG_INF = float("-inf")
BIG_ID = float(2 ** 24)
VMEM_LIMIT = 56 * 1024 * 1024


def _rms(x, g):
    return x * lax.rsqrt(jnp.mean(x * x, axis=-1, keepdims=True) + EPS) * g


def _dot_nt(a, b):
    return lax.dot_general(a, b, (((1,), (1,)), ((), ())), preferred_element_type=F32)


def _split_bf16(x):
    hi = x.astype(BF16)
    lo = (x - hi.astype(F32)).astype(BF16)
    return hi, lo


def _in_proj_kernel(x_ref, g_ref, w_ref, o_ref):
    h = _rms(x_ref[...], g_ref[...])
    o_ref[...] = jnp.dot(h.astype(BF16), w_ref[...], preferred_element_type=F32).astype(o_ref.dtype)


def _in_proj(x2, g, w, tm):
    n, d = x2.shape
    width = w.shape[1]
    return pl.pallas_call(
        _in_proj_kernel,
        grid=(n // tm,),
        in_specs=[pl.BlockSpec((tm, d), lambda i: (i, 0)),
                  pl.BlockSpec((1, d), lambda i: (0, 0)),
                  pl.BlockSpec((d, width), lambda i: (0, 0))],
        out_specs=pl.BlockSpec((tm, width), lambda i: (i, 0)),
        out_shape=jax.ShapeDtypeStruct((n, width), BF16),
        compiler_params=pltpu.CompilerParams(dimension_semantics=("parallel",),
                                             vmem_limit_bytes=VMEM_LIMIT),
        name="in_proj",
    )(x2, g, w)


def _conv_kernel(a_ref, gate_ref, w_ref, b_ref, lg_ref, lb_ref, o_ref, ubuf, *, tc, width):
    i = pl.program_id(1)

    @pl.when(i == 0)
    def _():
        ubuf[0:CONV_HALO, :] = jnp.zeros((CONV_HALO, ubuf.shape[1]), F32)

    @pl.when(i > 0)
    def _():
        ubuf[0:CONV_HALO, :] = ubuf[tc:tc + CONV_HALO, :]

    a = a_ref[...].astype(F32)
    gate = gate_ref[...].astype(F32)
    ubuf[CONV_HALO:CONV_HALO + tc, :] = a * jax.nn.sigmoid(gate)
    base = CONV_HALO - (width - 1)
    acc = jnp.zeros((tc, ubuf.shape[1]), F32)
    for j in range(width):
        acc = acc + w_ref[j:j + 1, :] * ubuf[base + j:base + j + tc, :]
    y = acc + b_ref[...]
    mu = jnp.mean(y, axis=-1, keepdims=True)
    yc = y - mu
    var = jnp.mean(yc * yc, axis=-1, keepdims=True)
    z = yc * lax.rsqrt(var + EPS) * lg_ref[...] + lb_ref[...]
    o_ref[...] = (z * jax.nn.sigmoid(z)).astype(o_ref.dtype)


def _conv(proj, conv_w, conv_b, ln_g, ln_b, batch, seq, tc):
    n = proj.shape[0]
    width, ch = conv_w.shape
    nt = seq // tc
    kern = functools.partial(_conv_kernel, tc=tc, width=width)
    vec = lambda: pl.BlockSpec((1, ch), lambda b, i: (0, 0))
    return pl.pallas_call(
        kern,
        grid=(batch, nt),
        in_specs=[pl.BlockSpec((tc, ch), lambda b, i: (b * nt + i, 0)),
                  pl.BlockSpec((tc, ch), lambda b, i: (b * nt + i, 1)),
                  pl.BlockSpec((width, ch), lambda b, i: (0, 0)),
                  vec(), vec(), vec()],
        out_specs=pl.BlockSpec((tc, ch), lambda b, i: (b * nt + i, 0)),
        out_shape=jax.ShapeDtypeStruct((n, ch), BF16),
        scratch_shapes=[pltpu.VMEM((tc + CONV_HALO, ch), F32)],
        compiler_params=pltpu.CompilerParams(dimension_semantics=("parallel", "arbitrary"),
                                             vmem_limit_bytes=VMEM_LIMIT),
        name="conv",
    )(proj, proj, conv_w, conv_b, ln_g, ln_b)


def _sb_kernel(q_ref, k_ref, v_ref, o_ref, *scratch, tq):
    n_chain = len(scratch) // 2
    acc_refs, carry_refs = scratch[:n_chain], scratch[n_chain:]
    qi = pl.program_id(2)
    q2 = q_ref[...]
    lane = lax.broadcasted_iota(jnp.int32, (1, LANES), 1)
    row = lax.broadcasted_iota(jnp.int32, (tq, tq), 0)
    col = lax.broadcasted_iota(jnp.int32, (tq, tq), 1)
    suffix = jnp.where(row > col, -1.0, 0.0).astype(BF16)
    scale = jnp.asarray(HEAD_DIM ** -0.5, q2.dtype)
    chains = []
    for h in range(2):
        head = (lane >= h * HEAD_DIM) & (lane < (h + 1) * HEAD_DIM)
        qh = jnp.where(head, q2, jnp.zeros_like(q2)) * scale
        for r0 in range(0, tq, SB_ROWS):
            chains.append((qh[r0:r0 + SB_ROWS], r0))
    for c in range(len(chains)):
        acc_refs[c][...] = jnp.zeros(acc_refs[c].shape, F32)
        carry_refs[c][...] = jnp.zeros(carry_refs[c].shape, F32)

    def stages(kb, diag):
        st = {}
        n = len(chains)

        def qk():
            start = pl.multiple_of(kb * tq, tq)
            st["k"] = k_ref[pl.ds(start, tq), :]
            st["v"] = v_ref[pl.ds(start, tq), :]
            st["z"] = [_dot_nt(qc, st["k"]) for qc, _ in chains]

        def logs():
            st["lg"], st["spb"], st["keep"] = [], [], []
            for z, (_, r0) in zip(st["z"], chains):
                lg = jnp.minimum(z, 0.0) - jnp.log(1.0 + jnp.exp(jnp.minimum(z, -z)))
                sp = z - lg
                if diag:
                    keep = (lax.broadcasted_iota(jnp.int32, (SB_ROWS, tq), 1)
                            < lax.broadcasted_iota(jnp.int32, (SB_ROWS, tq), 0) + r0)
                    sp = jnp.where(keep, sp, 0.0)
                    st["keep"].append(keep)
                st["lg"].append(lg)
                st["spb"].append(sp.astype(BF16))

        def sums():
            st["r"] = [jnp.dot(spb, suffix, preferred_element_type=F32) for spb in st["spb"]]

        def weights():
            st["a"] = []
            for c in range(n):
                after = st["r"][c] + carry_refs[c][...]
                a = jnp.exp(st["lg"][c] + after)
                if diag:
                    a = jnp.where(st["keep"][c], a, 0.0)
                st["a"].append(a.astype(BF16))
                carry_refs[c][...] = after[:, 0:1] - st["spb"][c][:, 0:LANES].astype(F32)[:, 0:1]

        def values():
            for c in range(n):
                acc_refs[c][...] += jnp.dot(st["a"][c], st["v"], preferred_element_type=F32)

        return [qk, logs, sums, weights, values]

    def run(blocks):
        seqs = [stages(kb, diag) for kb, diag in blocks]
        depth = len(seqs[0])
        for step in range(depth + len(seqs) - 1):
            for i, sq in enumerate(seqs):
                if 0 <= step - i < depth:
                    sq[step - i]()

    run([(qi, True)])

    def live(s):
        i, go = s
        return (i < qi // 2) & go

    def pair(s):
        i, _ = s
        run([(qi - 1 - 2 * i, False), (qi - 2 - 2 * i, False)])
        top = functools.reduce(jnp.maximum, [jnp.max(r[...]) for r in carry_refs])
        return i + 1, top > SB_DEAD

    _, go = lax.while_loop(live, pair, (jnp.int32(0), jnp.bool_(True)))

    @pl.when((qi % 2 == 1) & go)
    def _():
        run([(0, False)])

    n_sub = tq // SB_ROWS
    out = [jnp.concatenate([acc_refs[h * n_sub + s][...] for s in range(n_sub)], axis=0) for h in range(2)]
    o_ref[...] = jnp.where(lane < HEAD_DIM, out[0], out[1]).astype(o_ref.dtype)


def _stickbreak(proj, batch, seq, tq, q_col, k_col, v_col, n_pairs):
    n = proj.shape[0]
    nq = seq // tq
    kern = functools.partial(_sb_kernel, tq=tq)
    return pl.pallas_call(
        kern,
        grid=(batch, n_pairs, nq),
        in_specs=[pl.BlockSpec((tq, LANES), lambda b, p, i: (b * nq + i, q_col + p)),
                  pl.BlockSpec((seq, LANES), lambda b, p, i: (b, k_col + p)),
                  pl.BlockSpec((seq, LANES), lambda b, p, i: (b, v_col + p))],
        out_specs=pl.BlockSpec((tq, LANES), lambda b, p, i: (b * nq + i, p)),
        out_shape=jax.ShapeDtypeStruct((n, n_pairs * LANES), F32),
        scratch_shapes=([pltpu.VMEM((SB_ROWS, LANES), F32)] * (2 * tq // SB_ROWS)
                        + [pltpu.VMEM((SB_ROWS, 1), F32)] * (2 * tq // SB_ROWS)),
        compiler_params=pltpu.CompilerParams(
            dimension_semantics=("parallel", "parallel", "arbitrary"),
            vmem_limit_bytes=VMEM_LIMIT),
        name="stickbreak",
    )(proj, proj, proj)


def _head_sumsq(x, blk_ones):
    hi, lo = _split_bf16(x * x)
    return (jnp.dot(hi, blk_ones, preferred_element_type=F32)
            + jnp.dot(lo, blk_ones, preferred_element_type=F32))


def _block_ones(width):
    r = lax.broadcasted_iota(jnp.int32, (width, width), 0) // HEAD_DIM
    c = lax.broadcasted_iota(jnp.int32, (width, width), 1) // HEAD_DIM
    return jnp.where(r == c, 1.0, 0.0).astype(BF16)


def _mem_kv_kernel(mem_ref, g_ref, w_ref, kg_ref, k_ref, v_ref, *, width):
    mn = _rms(mem_ref[...], g_ref[...])
    kv = jnp.dot(mn.astype(BF16), w_ref[...], preferred_element_type=F32)
    k = kv[:, :width]
    ss = _head_sumsq(k, _block_ones(width))
    k_ref[...] = (k * lax.rsqrt(ss * (1.0 / HEAD_DIM) + EPS) * kg_ref[...]).astype(k_ref.dtype)
    v_ref[...] = kv[:, width:].astype(v_ref.dtype)


def _mem_kv(mem, g, w, kg_t):
    b, m, d = mem.shape
    width = w.shape[1] // 2
    kern = functools.partial(_mem_kv_kernel, width=width)
    out = jax.ShapeDtypeStruct((b, m, width), BF16)
    return pl.pallas_call(
        kern,
        grid=(b,),
        in_specs=[pl.BlockSpec((None, m, d), lambda i: (i, 0, 0)),
                  pl.BlockSpec((1, d), lambda i: (0, 0)),
                  pl.BlockSpec((d, 2 * width), lambda i: (0, 0)),
                  pl.BlockSpec((1, width), lambda i: (0, 0))],
        out_specs=[pl.BlockSpec((None, m, width), lambda i: (i, 0, 0)),
                   pl.BlockSpec((None, m, width), lambda i: (i, 0, 0))],
        out_shape=[out, out],
        compiler_params=pltpu.CompilerParams(dimension_semantics=("parallel",),
                                             vmem_limit_bytes=VMEM_LIMIT),
        name="mem_kv",
    )(mem, g, w, kg_t)


def _mem_attn_kernel(q_ref, k_ref, v_ref, qg_ref, og_ref, o_ref, *, width):
    q = q_ref[...].astype(F32)
    ss = _head_sumsq(q, _block_ones(width))
    qn = q * lax.rsqrt(ss * (1.0 / HEAD_DIM) + EPS) * qg_ref[...]
    k = k_ref[...]
    v = v_ref[...]
    lane = lax.broadcasted_iota(jnp.int32, (1, width), 1)
    out = jnp.zeros(q.shape, F32)
    for h in range(width // HEAD_DIM):
        head = (lane >= h * HEAD_DIM) & (lane < (h + 1) * HEAD_DIM)
        qh = jnp.where(head, qn, 0.0).astype(BF16)
        s = _dot_nt(qh, k) * (HEAD_DIM ** -0.5)
        p = jnp.exp(s - jnp.max(s, axis=-1, keepdims=True))
        o = jnp.dot(p.astype(BF16), v, preferred_element_type=F32) / jnp.sum(p, axis=-1, keepdims=True)
        out = jnp.where(head, o, out)
    o_ref[...] = _rms(out, og_ref[...]).astype(o_ref.dtype)


def _mem_attn(proj, k, v, qg_t, og, batch, seq, tm, q_col):
    n = proj.shape[0]
    _, m, width = k.shape
    nt = seq // tm
    kern = functools.partial(_mem_attn_kernel, width=width)
    return pl.pallas_call(
        kern,
        grid=(batch, nt),
        in_specs=[pl.BlockSpec((tm, width), lambda b, i: (b * nt + i, q_col)),
                  pl.BlockSpec((None, m, width), lambda b, i: (b, 0, 0)),
                  pl.BlockSpec((None, m, width), lambda b, i: (b, 0, 0)),
                  pl.BlockSpec((1, width), lambda b, i: (0, 0)),
                  pl.BlockSpec((1, width), lambda b, i: (0, 0))],
        out_specs=pl.BlockSpec((tm, width), lambda b, i: (b * nt + i, 0)),
        out_shape=jax.ShapeDtypeStruct((n, width), BF16),
        compiler_params=pltpu.CompilerParams(dimension_semantics=("parallel", "parallel"),
                                             vmem_limit_bytes=VMEM_LIMIT),
        name="mem_attn",
    )(proj, k, v, qg_t, og)


def _out_proj_kernel(conv_ref, sb_ref, mem_ref, x_ref, sbg_ref, wo_ref, fg_ref, wq_ref,
                     x1_ref, h2_ref, q_ref, *, c_w, s_w):
    sbn = _rms(sb_ref[...], sbg_ref[...]).astype(BF16)
    mixed = jnp.dot(conv_ref[...], wo_ref[0:c_w, :], preferred_element_type=F32)
    mixed = mixed + jnp.dot(sbn, wo_ref[c_w:c_w + s_w, :], preferred_element_type=F32)
    mixed = mixed + jnp.dot(mem_ref[...], wo_ref[c_w + s_w:, :], preferred_element_type=F32)
    x1 = x_ref[...] + mixed
    h2 = _rms(x1, fg_ref[...])
    x1_ref[...] = x1
    h2_ref[...] = h2
    q_ref[...] = jnp.dot(h2.astype(BF16), wq_ref[...], preferred_element_type=F32).astype(q_ref.dtype)


def _out_proj(conv_o, sb, mem_o, x2, sbg, w_out, fg, w_q, tm):
    n, d = x2.shape
    c_w, s_w, m_w = conv_o.shape[1], sb.shape[1], mem_o.shape[1]
    qw = w_q.shape[1]
    kern = functools.partial(_out_proj_kernel, c_w=c_w, s_w=s_w)
    row = lambda w: pl.BlockSpec((tm, w), lambda i: (i, 0))
    full = lambda r, c: pl.BlockSpec((r, c), lambda i: (0, 0))
    return pl.pallas_call(
        kern,
        grid=(n // tm,),
        in_specs=[row(c_w), row(s_w), row(m_w), row(d), full(1, s_w), full(c_w + s_w + m_w, d),
                  full(1, d), full(d, qw)],
        out_specs=[row(d), row(d), row(qw)],
        out_shape=[jax.ShapeDtypeStruct((n, d), F32), jax.ShapeDtypeStruct((n, d), F32),
                   jax.ShapeDtypeStruct((n, qw), BF16)],
        compiler_params=pltpu.CompilerParams(dimension_semantics=("parallel",),
                                             vmem_limit_bytes=VMEM_LIMIT),
        name="out_proj",
    )(conv_o, sb, mem_o, x2, sbg, w_out, fg, w_q)


def _topk_rows(s, ids, k):
    vals, sel = [], []
    for _ in range(k):
        m = jnp.max(s, axis=0, keepdims=True)
        i = jnp.min(jnp.where(s == m, ids, BIG_ID), axis=0, keepdims=True)
        s = jnp.where(ids == i, NEG_INF, s)
        vals.append(m)
        sel.append(i)
    return jnp.concatenate(vals, axis=0), jnp.concatenate(sel, axis=0)


def _peer_topk_kernel(q_ref, k1_ref, k2_ref, eidx_ref, gates_ref, *, half):
    t = q_ref.shape[0]
    key_ids = lax.broadcasted_iota(jnp.int32, (N_KEYS, t), 0).astype(F32)
    q = q_ref[...]
    v1, i1 = _topk_rows(_dot_nt(k1_ref[...], q[:, :half]), key_ids, PEER_TOPK)
    v2, i2 = _topk_rows(_dot_nt(k2_ref[...], q[:, half:]), key_ids, PEER_TOPK)
    cand_s, cand_i = [v1[0:1] + v2], [i1[0:1] * N_KEYS + i2]
    sub = lax.broadcasted_iota(jnp.int32, (8, t), 0)
    for a in range(1, 8):
        ok = sub < (PEER_TOPK // (a + 1))
        cand_s.append(jnp.where(ok, v1[a:a + 1] + v2[0:8], NEG_INF))
        cand_i.append(i1[a:a + 1] * N_KEYS + i2[0:8])
    cand_s.append(v1[8:16] + v2[0:1])
    cand_i.append(i1[8:16] * N_KEYS + i2[0:1])
    top_s, eidx = _topk_rows(jnp.concatenate(cand_s, axis=0), jnp.concatenate(cand_i, axis=0), PEER_TOPK)
    p = jnp.exp(top_s - top_s[0:1])
    eidx_ref[...] = eidx.astype(jnp.int32)
    gates_ref[...] = p / jnp.sum(p, axis=0, keepdims=True)


def _peer_topk(q, k1, k2, tt):
    n, qw = q.shape
    heads, n_keys, half = k1.shape
    kern = functools.partial(_peer_topk_kernel, half=half)
    out_spec = lambda: pl.BlockSpec((PEER_TOPK, tt), lambda i, h: (h, i))
    return pl.pallas_call(
        kern,
        grid=(n // tt, heads),
        in_specs=[pl.BlockSpec((tt, 2 * half), lambda i, h: (i, h)),
                  pl.BlockSpec((None, n_keys, half), lambda i, h: (h, 0, 0)),
                  pl.BlockSpec((None, n_keys, half), lambda i, h: (h, 0, 0))],
        out_specs=[out_spec(), out_spec()],
        out_shape=[jax.ShapeDtypeStruct((heads * PEER_TOPK, n), jnp.int32),
                   jax.ShapeDtypeStruct((heads * PEER_TOPK, n), F32)],
        compiler_params=pltpu.CompilerParams(dimension_semantics=("parallel", "parallel"),
                                             vmem_limit_bytes=VMEM_LIMIT),
        name="peer_topk",
    )(q, k1, k2)


def _unpack_pair(w):
    lo = pltpu.bitcast(w << 16, F32)
    hi = pltpu.bitcast(w & jnp.uint32(0xFFFF0000), F32)
    return lo, hi


def _peer_u_kernel(idx_ref, tbl_ref, h2_ref, gates_ref, coef_ref, act_ref, *y_refs, tg, picks):
    lane = lax.broadcasted_iota(jnp.int32, (1, tg), 1)
    group = len(y_refs)

    def products(t, y_ref):
        hv = h2_ref[t]
        h_lo, h_hi = hv[0:4], hv[4:8]
        for p in range(picks):
            lo, hi = _unpack_pair(tbl_ref[idx_ref[t, p]])
            y_ref[4 * p:4 * p + 4, :] = lo * h_lo + hi * h_hi

    def reduce(t, y_ref):
        part = (y_ref[pl.ds(0, picks, stride=4), :] + y_ref[pl.ds(1, picks, stride=4), :]
                + y_ref[pl.ds(2, picks, stride=4), :] + y_ref[pl.ds(3, picks, stride=4), :])
        act = jnp.sum(part, axis=1, keepdims=True)
        act_ref[...] = jnp.where(lane == t, act, act_ref[...])

    y_refs[-1][...] = jnp.zeros(y_refs[-1].shape, F32)
    act_ref[...] = jnp.zeros(act_ref.shape, F32)

    def body(j, c):
        t0 = group * j
        for k in range(group):
            products(t0 + k, y_refs[k])
            reduce(t0 + k - 1, y_refs[k - 1])
        return c

    lax.fori_loop(0, tg // group, body, 0)
    reduce(tg - 1, y_refs[-1])
    act = act_ref[...]
    gelu = 0.5 * act * (1.0 + lax.erf(act * (2.0 ** -0.5)))
    coef_ref[...] = gates_ref[...] * gelu


def _peer_u(eidx, tbl, h2r, gates, tg):
    n, picks = eidx.shape
    kern = functools.partial(_peer_u_kernel, tg=tg, picks=picks)
    return pl.pallas_call(
        kern,
        grid=(n // tg,),
        in_specs=[pl.BlockSpec((tg, picks), lambda i: (i, 0), memory_space=pltpu.SMEM,
                               pipeline_mode=pl.Buffered(1)),
                  pl.BlockSpec(memory_space=pltpu.VMEM),
                  pl.BlockSpec((tg, 8, LANES), lambda i: (i, 0, 0)),
                  pl.BlockSpec((picks, tg), lambda i: (0, i))],
        out_specs=pl.BlockSpec((picks, tg), lambda i: (0, i)),
        out_shape=jax.ShapeDtypeStruct((picks, n), F32),
        scratch_shapes=([pltpu.VMEM((picks, tg), F32)]
                        + [pltpu.VMEM((4 * picks, LANES), F32)] * PEER_U_GROUP),
        compiler_params=pltpu.CompilerParams(dimension_semantics=("parallel",),
                                             vmem_limit_bytes=VMEM_LIMIT),
        name="peer_u",
    )(eidx, tbl, h2r, gates)


def _sc_gather_rows(table, idx):
    b, w = idx.shape[0], table.shape[1]
    workers = SC_CORES * SC_SUBCORES
    per_w = b // workers
    step = 2 * SC_ROWS
    assert b % workers == 0 and per_w % SC_IDX == 0 and SC_IDX % step == 0
    mesh = plsc.VectorSubcoreMesh(core_axis_name="c", subcore_axis_name="s")

    def body(table_hbm, idx_hbm, out_hbm, idx_v, rows0, rows1, g0, g1, w0, w1):
        wid = lax.axis_index("s") * SC_CORES + lax.axis_index("c")
        base = wid * per_w

        @pl.loop(0, per_w // SC_IDX)
        def _(o):
            off = pl.multiple_of(base + o * SC_IDX, SC_IDX)
            pltpu.sync_copy(idx_hbm.at[pl.ds(off, SC_IDX)], idx_v)

            @pl.loop(0, SC_IDX // step)
            def _(j):
                r0 = pl.multiple_of(j * step, step)
                r1 = pl.multiple_of(r0 + SC_ROWS, SC_ROWS)
                ga = pltpu.async_copy(table_hbm.at[idx_v.at[pl.ds(r0, SC_ROWS)]], rows0, g0)
                gb = pltpu.async_copy(table_hbm.at[idx_v.at[pl.ds(r1, SC_ROWS)]], rows1, g1)
                ga.wait()
                wa = pltpu.async_copy(rows0, out_hbm.at[pl.ds(pl.multiple_of(off + r0, SC_ROWS), SC_ROWS)], w0)
                gb.wait()
                wb = pltpu.async_copy(rows1, out_hbm.at[pl.ds(pl.multiple_of(off + r1, SC_ROWS), SC_ROWS)], w1)
                wa.wait()
                wb.wait()

    return pl.kernel(
        body,
        out_type=jax.ShapeDtypeStruct((b, w), table.dtype),
        mesh=mesh,
        scratch_types=[pltpu.VMEM((SC_IDX,), jnp.int32), pltpu.VMEM((SC_ROWS, w), table.dtype),
                       pltpu.VMEM((SC_ROWS, w), table.dtype)] + [pltpu.SemaphoreType.DMA] * 4,
        name="sc_gather_rows",
    )(table, idx)


def _peer_v_dense_kernel(rows_ref, coef_ref, x1_ref, o_ref, *, tt, picks):
    half = o_ref.shape[1] // 2
    hi, lo = _split_bf16(coef_ref[...])
    pr = lax.broadcasted_iota(jnp.int32, (picks, 2 * picks), 0)
    pc = lax.broadcasted_iota(jnp.int32, (picks, 2 * picks), 1)
    even = jnp.where(pc == 2 * pr, 1.0, 0.0).astype(BF16)
    odd = jnp.where(pc == 2 * pr + 1, 1.0, 0.0).astype(BF16)
    parts = [jnp.dot(x, s, preferred_element_type=F32) for s in (even, odd) for x in (hi, lo)]
    sub = lax.broadcasted_iota(jnp.int32, (16, 2 * picks), 0)
    for t in range(tt):
        b = pltpu.bitcast(rows_ref[t * picks:(t + 1) * picks, :], BF16)
        lhs = jnp.zeros((16, 2 * picks), F32)
        for k, part in enumerate(parts):
            lhs = jnp.where(sub == k, part[t:t + 1, :], lhs)
        r = jnp.dot(lhs.astype(BF16), b, preferred_element_type=F32)
        o_ref[t:t + 1, :half] = x1_ref[t:t + 1, :half] + (r[0:1] + r[1:2])
        o_ref[t:t + 1, half:] = x1_ref[t:t + 1, half:] + (r[2:3] + r[3:4])


def _peer_v_dense(rows, coef_t, x1, tt):
    n, picks = coef_t.shape
    d = x1.shape[1]
    kern = functools.partial(_peer_v_dense_kernel, tt=tt, picks=picks)
    return pl.pallas_call(
        kern,
        grid=(n // tt,),
        in_specs=[pl.BlockSpec((tt * picks, d // 2), lambda i: (i, 0)),
                  pl.BlockSpec((tt, picks), lambda i: (i, 0)),
                  pl.BlockSpec((tt, d), lambda i: (i, 0))],
        out_specs=pl.BlockSpec((tt, d), lambda i: (i, 0)),
        out_shape=jax.ShapeDtypeStruct((n, d), F32),
        compiler_params=pltpu.CompilerParams(dimension_semantics=("parallel",),
                                             vmem_limit_bytes=VMEM_LIMIT),
        name="peer_v_dense",
    )(rows, coef_t, x1)


def _pack_table(t):
    e, d = t.shape
    tb = lax.bitcast_convert_type(t.astype(BF16), jnp.uint16).astype(jnp.uint32)
    packed = tb[:, :d // 2] | (tb[:, d // 2:] << 16)
    return packed.reshape(e, d // (2 * LANES), LANES)


def _tile(n, pref):
    t = min(n, pref)
    assert n % t == 0, (n, pref)
    return t


def kernel(x, mem, attn_norm_g, w_in, conv_w, conv_b, conv_ln_g, conv_ln_b, sb_out_g, mem_norm_g, w_mem_kv, q_norm_g, k_norm_g, mem_out_g, w_out, ffn_norm_g, w_peer_q, peer_keys1, peer_keys2, peer_u, peer_v):
    batch, seq, d = x.shape
    depth = w_in.shape[0]
    conv_ch = conv_w.shape[2]
    sb_width = sb_out_g.shape[1]
    mem_width = mem_out_g.shape[1]
    assert d == 8 * LANES and conv_ch % LANES == 0 and sb_width % LANES == 0
    n = batch * seq
    q_off, k_off, v_off = 2 * conv_ch, 2 * conv_ch + sb_width, 2 * conv_ch + 2 * sb_width
    m_off = 2 * conv_ch + 3 * sb_width
    row = lambda v: v.reshape(1, -1)

    x2 = x.reshape(n, d)
    for l in range(depth):
        proj = _in_proj(x2, row(attn_norm_g[l]), w_in[l].astype(BF16), _tile(n, 512))
        conv_o = _conv(proj, conv_w[l], row(conv_b[l]), row(conv_ln_g[l]), row(conv_ln_b[l]),
                       batch, seq, _tile(seq, 512))
        sb = _stickbreak(proj, batch, seq, _tile(seq, 256), q_off // LANES, k_off // LANES,
                         v_off // LANES, sb_width // LANES)
        heads_m = mem_width // HEAD_DIM
        mk, mv = _mem_kv(mem, row(mem_norm_g[l]), w_mem_kv[l].astype(BF16),
                         row(jnp.tile(k_norm_g[l], heads_m)))
        mem_o = _mem_attn(proj, mk, mv, row(jnp.tile(q_norm_g[l], heads_m)), row(mem_out_g[l]),
                          batch, seq, _tile(seq, 512), m_off // mem_width)
        x1, h2, q = _out_proj(conv_o, sb, mem_o, x2, row(sb_out_g[l]), w_out[l].astype(BF16),
                              row(ffn_norm_g[l]), w_peer_q[l].astype(BF16), _tile(n, 512))
        eidx, gates = _peer_topk(q, peer_keys1[l].astype(BF16), peer_keys2[l].astype(BF16),
                                 _tile(n, 256))
        tg = _tile(n, 128)
        eidx_t = eidx.T
        v_rows = _sc_gather_rows(_pack_table(peer_v[l]).reshape(peer_v.shape[1], d // 2),
                                 eidx_t.reshape(-1))
        coef = _peer_u(eidx_t, _pack_table(peer_u[l]), h2.reshape(n, 8, LANES), gates, tg)
        x2 = _peer_v_dense(v_rows, coef.T, x1, _tile(n, 32))
    return x2.reshape(batch, seq, d)
```

```python
import functools

import jax
import jax.numpy as jnp
from jax import lax
from jax.experimental import pallas as pl
from jax.experimental.pallas import tpu as pltpu
from jax.experimental.pallas import tpu_sc as plsc

F32 = jnp.float32
BF16 = jnp.bfloat16
EPS = 1e-6

HEAD_DIM = 64
LANES = 128
CONV_HALO = 32
SB_ROWS = 128
SB_DEAD = -120.0
SC_CORES, SC_SUBCORES = 2, 16
SC_ROWS = 64
SC_IDX = 4096
PEER_U_GROUP = 8
PEER_DENSE_TOKENS = 16
PEER_TOPK = 16
N_KEYS = 128
NEG_INF = float("-inf")
BIG_ID = float(2 ** 24)
VMEM_LIMIT = 56 * 1024 * 1024


def _rms(x, g):
    return x * lax.rsqrt(jnp.mean(x * x, axis=-1, keepdims=True) + EPS) * g


def _dot_nt(a, b):
    return lax.dot_general(a, b, (((1,), (1,)), ((), ())), preferred_element_type=F32)


def _split_bf16(x):
    hi = x.astype(BF16)
    lo = (x - hi.astype(F32)).astype(BF16)
    return hi, lo


def _in_proj_kernel(x_ref, g_ref, w_ref, o_ref):
    h = _rms(x_ref[...], g_ref[...])
    o_ref[...] = jnp.dot(h.astype(BF16), w_ref[...], preferred_element_type=F32).astype(o_ref.dtype)


def _in_proj(x2, g, w, tm):
    n, d = x2.shape
    width = w.shape[1]
    return pl.pallas_call(
        _in_proj_kernel,
        grid=(n // tm,),
        in_specs=[pl.BlockSpec((tm, d), lambda i: (i, 0)),
                  pl.BlockSpec((1, d), lambda i: (0, 0)),
                  pl.BlockSpec((d, width), lambda i: (0, 0))],
        out_specs=pl.BlockSpec((tm, width), lambda i: (i, 0)),
        out_shape=jax.ShapeDtypeStruct((n, width), BF16),
        compiler_params=pltpu.CompilerParams(dimension_semantics=("parallel",),
                                             vmem_limit_bytes=VMEM_LIMIT),
        name="in_proj",
    )(x2, g, w)


def _conv_kernel(a_ref, gate_ref, w_ref, b_ref, lg_ref, lb_ref, o_ref, ubuf, *, tc, width):
    i = pl.program_id(1)

    @pl.when(i == 0)
    def _():
        ubuf[0:CONV_HALO, :] = jnp.zeros((CONV_HALO, ubuf.shape[1]), F32)

    @pl.when(i > 0)
    def _():
        ubuf[0:CONV_HALO, :] = ubuf[tc:tc + CONV_HALO, :]

    a = a_ref[...].astype(F32)
    gate = gate_ref[...].astype(F32)
    ubuf[CONV_HALO:CONV_HALO + tc, :] = a * jax.nn.sigmoid(gate)
    base = CONV_HALO - (width - 1)
    acc = jnp.zeros((tc, ubuf.shape[1]), F32)
    for j in range(width):
        acc = acc + w_ref[j:j + 1, :] * ubuf[base + j:base + j + tc, :]
    y = acc + b_ref[...]
    mu = jnp.mean(y, axis=-1, keepdims=True)
    yc = y - mu
    var = jnp.mean(yc * yc, axis=-1, keepdims=True)
    z = yc * lax.rsqrt(var + EPS) * lg_ref[...] + lb_ref[...]
    o_ref[...] = (z * jax.nn.sigmoid(z)).astype(o_ref.dtype)


def _conv(proj, conv_w, conv_b, ln_g, ln_b, batch, seq, tc):
    n = proj.shape[0]
    width, ch = conv_w.shape
    nt = seq // tc
    kern = functools.partial(_conv_kernel, tc=tc, width=width)
    vec = lambda: pl.BlockSpec((1, ch), lambda b, i: (0, 0))
    return pl.pallas_call(
        kern,
        grid=(batch, nt),
        in_specs=[pl.BlockSpec((tc, ch), lambda b, i: (b * nt + i, 0)),
                  pl.BlockSpec((tc, ch), lambda b, i: (b * nt + i, 1)),
                  pl.BlockSpec((width, ch), lambda b, i: (0, 0)),
                  vec(), vec(), vec()],
        out_specs=pl.BlockSpec((tc, ch), lambda b, i: (b * nt + i, 0)),
        out_shape=jax.ShapeDtypeStruct((n, ch), BF16),
        scratch_shapes=[pltpu.VMEM((tc + CONV_HALO, ch), F32)],
        compiler_params=pltpu.CompilerParams(dimension_semantics=("parallel", "arbitrary"),
                                             vmem_limit_bytes=VMEM_LIMIT),
        name="conv",
    )(proj, proj, conv_w, conv_b, ln_g, ln_b)


def _sb_kernel(q_ref, k_ref, v_ref, o_ref, *scratch, tq):
    n_chain = len(scratch) // 2
    acc_refs, carry_refs = scratch[:n_chain], scratch[n_chain:]
    qi = pl.program_id(2)
    q2 = q_ref[...]
    lane = lax.broadcasted_iota(jnp.int32, (1, LANES), 1)
    row = lax.broadcasted_iota(jnp.int32, (tq, tq), 0)
    col = lax.broadcasted_iota(jnp.int32, (tq, tq), 1)
    suffix = jnp.where(row > col, -1.0, 0.0).astype(BF16)
    scale = jnp.asarray(HEAD_DIM ** -0.5, q2.dtype)
    chains = []
    for h in range(2):
        head = (lane >= h * HEAD_DIM) & (lane < (h + 1) * HEAD_DIM)
        qh = jnp.where(head, q2, jnp.zeros_like(q2)) * scale
        for r0 in range(0, tq, SB_ROWS):
            chains.append((qh[r0:r0 + SB_ROWS], r0))
    for c in range(len(chains)):
        acc_refs[c][...] = jnp.zeros(acc_refs[c].shape, F32)
        carry_refs[c][...] = jnp.zeros(carry_refs[c].shape, F32)

    def stages(kb, diag):
        st = {}
        n = len(chains)

        def qk():
            start = pl.multiple_of(kb * tq, tq)
            st["k"] = k_ref[pl.ds(start, tq), :]
            st["v"] = v_ref[pl.ds(start, tq), :]
            st["z"] = [_dot_nt(qc, st["k"]) for qc, _ in chains]

        def logs():
            st["lg"], st["spb"], st["keep"] = [], [], []
            for z, (_, r0) in zip(st["z"], chains):
                lg = jnp.minimum(z, 0.0) - jnp.log(1.0 + jnp.exp(jnp.minimum(z, -z)))
                sp = z - lg
                if diag:
                    keep = (lax.broadcasted_iota(jnp.int32, (SB_ROWS, tq), 1)
                            < lax.broadcasted_iota(jnp.int32, (SB_ROWS, tq), 0) + r0)
                    sp = jnp.where(keep, sp, 0.0)
                    st["keep"].append(keep)
                st["lg"].append(lg)
                st["spb"].append(sp.astype(BF16))

        def sums():
            st["r"] = [jnp.dot(spb, suffix, preferred_element_type=F32) for spb in st["spb"]]

        def weights():
            st["a"] = []
            for c in range(n):
                after = st["r"][c] + carry_refs[c][...]
                a = jnp.exp(st["lg"][c] + after)
                if diag:
                    a = jnp.where(st["keep"][c], a, 0.0)
                st["a"].append(a.astype(BF16))
                carry_refs[c][...] = after[:, 0:1] - st["spb"][c][:, 0:LANES].astype(F32)[:, 0:1]

        def values():
            for c in range(n):
                acc_refs[c][...] += jnp.dot(st["a"][c], st["v"], preferred_element_type=F32)

        return [qk, logs, sums, weights, values]

    def run(blocks):
        seqs = [stages(kb, diag) for kb, diag in blocks]
        depth = len(seqs[0])
        for step in range(depth + len(seqs) - 1):
            for i, sq in enumerate(seqs):
                if 0 <= step - i < depth:
                    sq[step - i]()

    run([(qi, True)])

    def live(s):
        i, go = s
        return (i < qi // 2) & go

    def pair(s):
        i, _ = s
        run([(qi - 1 - 2 * i, False), (qi - 2 - 2 * i, False)])
        top = functools.reduce(jnp.maximum, [jnp.max(r[...]) for r in carry_refs])
        return i + 1, top > SB_DEAD

    _, go = lax.while_loop(live, pair, (jnp.int32(0), jnp.bool_(True)))

    @pl.when((qi % 2 == 1) & go)
    def _():
        run([(0, False)])

    n_sub = tq // SB_ROWS
    out = [jnp.concatenate([acc_refs[h * n_sub + s][...] for s in range(n_sub)], axis=0) for h in range(2)]
    o_ref[...] = jnp.where(lane < HEAD_DIM, out[0], out[1]).astype(o_ref.dtype)


def _stickbreak(proj, batch, seq, tq, q_col, k_col, v_col, n_pairs):
    n = proj.shape[0]
    nq = seq // tq
    kern = functools.partial(_sb_kernel, tq=tq)
    return pl.pallas_call(
        kern,
        grid=(batch, n_pairs, nq),
        in_specs=[pl.BlockSpec((tq, LANES), lambda b, p, i: (b * nq + i, q_col + p)),
                  pl.BlockSpec((seq, LANES), lambda b, p, i: (b, k_col + p)),
                  pl.BlockSpec((seq, LANES), lambda b, p, i: (b, v_col + p))],
        out_specs=pl.BlockSpec((tq, LANES), lambda b, p, i: (b * nq + i, p)),
        out_shape=jax.ShapeDtypeStruct((n, n_pairs * LANES), F32),
        scratch_shapes=([pltpu.VMEM((SB_ROWS, LANES), F32)] * (2 * tq // SB_ROWS)
                        + [pltpu.VMEM((SB_ROWS, 1), F32)] * (2 * tq // SB_ROWS)),
        compiler_params=pltpu.CompilerParams(
            dimension_semantics=("parallel", "parallel", "arbitrary"),
            vmem_limit_bytes=VMEM_LIMIT),
        name="stickbreak",
    )(proj, proj, proj)


def _head_sumsq(x, blk_ones):
    hi, lo = _split_bf16(x * x)
    return (jnp.dot(hi, blk_ones, preferred_element_type=F32)
            + jnp.dot(lo, blk_ones, preferred_element_type=F32))


def _block_ones(width):
    r = lax.broadcasted_iota(jnp.int32, (width, width), 0) // HEAD_DIM
    c = lax.broadcasted_iota(jnp.int32, (width, width), 1) // HEAD_DIM
    return jnp.where(r == c, 1.0, 0.0).astype(BF16)


def _mem_kv_kernel(mem_ref, g_ref, w_ref, kg_ref, k_ref, v_ref, *, width):
    mn = _rms(mem_ref[...], g_ref[...])
    kv = jnp.dot(mn.astype(BF16), w_ref[...], preferred_element_type=F32)
    k = kv[:, :width]
    ss = _head_sumsq(k, _block_ones(width))
    k_ref[...] = (k * lax.rsqrt(ss * (1.0 / HEAD_DIM) + EPS) * kg_ref[...]).astype(k_ref.dtype)
    v_ref[...] = kv[:, width:].astype(v_ref.dtype)


def _mem_kv(mem, g, w, kg_t):
    b, m, d = mem.shape
    width = w.shape[1] // 2
    kern = functools.partial(_mem_kv_kernel, width=width)
    out = jax.ShapeDtypeStruct((b, m, width), BF16)
    return pl.pallas_call(
        kern,
        grid=(b,),
        in_specs=[pl.BlockSpec((None, m, d), lambda i: (i, 0, 0)),
                  pl.BlockSpec((1, d), lambda i: (0, 0)),
                  pl.BlockSpec((d, 2 * width), lambda i: (0, 0)),
                  pl.BlockSpec((1, width), lambda i: (0, 0))],
        out_specs=[pl.BlockSpec((None, m, width), lambda i: (i, 0, 0)),
                   pl.BlockSpec((None, m, width), lambda i: (i, 0, 0))],
        out_shape=[out, out],
        compiler_params=pltpu.CompilerParams(dimension_semantics=("parallel",),
                                             vmem_limit_bytes=VMEM_LIMIT),
        name="mem_kv",
    )(mem, g, w, kg_t)


def _mem_attn_kernel(q_ref, k_ref, v_ref, qg_ref, og_ref, o_ref, *, width):
    q = q_ref[...].astype(F32)
    ss = _head_sumsq(q, _block_ones(width))
    qn = q * lax.rsqrt(ss * (1.0 / HEAD_DIM) + EPS) * qg_ref[...]
    k = k_ref[...]
    v = v_ref[...]
    lane = lax.broadcasted_iota(jnp.int32, (1, width), 1)
    out = jnp.zeros(q.shape, F32)
    for h in range(width // HEAD_DIM):
        head = (lane >= h * HEAD_DIM) & (lane < (h + 1) * HEAD_DIM)
        qh = jnp.where(head, qn, 0.0).astype(BF16)
        s = _dot_nt(qh, k) * (HEAD_DIM ** -0.5)
        p = jnp.exp(s - jnp.max(s, axis=-1, keepdims=True))
        o = jnp.dot(p.astype(BF16), v, preferred_element_type=F32) / jnp.sum(p, axis=-1, keepdims=True)
        out = jnp.where(head, o, out)
    o_ref[...] = _rms(out, og_ref[...]).astype(o_ref.dtype)


def _mem_attn(proj, k, v, qg_t, og, batch, seq, tm, q_col):
    n = proj.shape[0]
    _, m, width = k.shape
    nt = seq // tm
    kern = functools.partial(_mem_attn_kernel, width=width)
    return pl.pallas_call(
        kern,
        grid=(batch, nt),
        in_specs=[pl.BlockSpec((tm, width), lambda b, i: (b * nt + i, q_col)),
                  pl.BlockSpec((None, m, width), lambda b, i: (b, 0, 0)),
                  pl.BlockSpec((None, m, width), lambda b, i: (b, 0, 0)),
                  pl.BlockSpec((1, width), lambda b, i: (0, 0)),
                  pl.BlockSpec((1, width), lambda b, i: (0, 0))],
        out_specs=pl.BlockSpec((tm, width), lambda b, i: (b * nt + i, 0)),
        out_shape=jax.ShapeDtypeStruct((n, width), BF16),
        compiler_params=pltpu.CompilerParams(dimension_semantics=("parallel", "parallel"),
                                             vmem_limit_bytes=VMEM_LIMIT),
        name="mem_attn",
    )(proj, k, v, qg_t, og)


def _out_proj_kernel(conv_ref, sb_ref, mem_ref, x_ref, sbg_ref, wo_ref, fg_ref, wq_ref,
                     x1_ref, h2_ref, q_ref, *, c_w, s_w):
    sbn = _rms(sb_ref[...], sbg_ref[...]).astype(BF16)
    mixed = jnp.dot(conv_ref[...], wo_ref[0:c_w, :], preferred_element_type=F32)
    mixed = mixed + jnp.dot(sbn, wo_ref[c_w:c_w + s_w, :], preferred_element_type=F32)
    mixed = mixed + jnp.dot(mem_ref[...], wo_ref[c_w + s_w:, :], preferred_element_type=F32)
    x1 = x_ref[...] + mixed
    h2 = _rms(x1, fg_ref[...])
    x1_ref[...] = x1
    h2_ref[...] = h2
    q_ref[...] = jnp.dot(h2.astype(BF16), wq_ref[...], preferred_element_type=F32).astype(q_ref.dtype)


def _out_proj(conv_o, sb, mem_o, x2, sbg, w_out, fg, w_q, tm):
    n, d = x2.shape
    c_w, s_w, m_w = conv_o.shape[1], sb.shape[1], mem_o.shape[1]
    qw = w_q.shape[1]
    kern = functools.partial(_out_proj_kernel, c_w=c_w, s_w=s_w)
    row = lambda w: pl.BlockSpec((tm, w), lambda i: (i, 0))
    full = lambda r, c: pl.BlockSpec((r, c), lambda i: (0, 0))
    return pl.pallas_call(
        kern,
        grid=(n // tm,),
        in_specs=[row(c_w), row(s_w), row(m_w), row(d), full(1, s_w), full(c_w + s_w + m_w, d),
                  full(1, d), full(d, qw)],
        out_specs=[row(d), row(d), row(qw)],
        out_shape=[jax.ShapeDtypeStruct((n, d), F32), jax.ShapeDtypeStruct((n, d), F32),
                   jax.ShapeDtypeStruct((n, qw), BF16)],
        compiler_params=pltpu.CompilerParams(dimension_semantics=("parallel",),
                                             vmem_limit_bytes=VMEM_LIMIT),
        name="out_proj",
    )(conv_o, sb, mem_o, x2, sbg, w_out, fg, w_q)


def _topk_rows(s, ids, k):
    vals, sel = [], []
    for _ in range(k):
        m = jnp.max(s, axis=0, keepdims=True)
        i = jnp.min(jnp.where(s == m, ids, BIG_ID), axis=0, keepdims=True)
        s = jnp.where(ids == i, NEG_INF, s)
        vals.append(m)
        sel.append(i)
    return jnp.concatenate(vals, axis=0), jnp.concatenate(sel, axis=0)


def _peer_topk_kernel(q_ref, k1_ref, k2_ref, eidx_ref, gates_ref, *, half):
    t = q_ref.shape[0]
    key_ids = lax.broadcasted_iota(jnp.int32, (N_KEYS, t), 0).astype(F32)
    q = q_ref[...]
    v1, i1 = _topk_rows(_dot_nt(k1_ref[...], q[:, :half]), key_ids, PEER_TOPK)
    v2, i2 = _topk_rows(_dot_nt(k2_ref[...], q[:, half:]), key_ids, PEER_TOPK)
    cand_s, cand_i = [v1[0:1] + v2], [i1[0:1] * N_KEYS + i2]
    sub = lax.broadcasted_iota(jnp.int32, (8, t), 0)
    for a in range(1, 8):
        ok = sub < (PEER_TOPK // (a + 1))
        cand_s.append(jnp.where(ok, v1[a:a + 1] + v2[0:8], NEG_INF))
        cand_i.append(i1[a:a + 1] * N_KEYS + i2[0:8])
    cand_s.append(v1[8:16] + v2[0:1])
    cand_i.append(i1[8:16] * N_KEYS + i2[0:1])
    top_s, eidx = _topk_rows(jnp.concatenate(cand_s, axis=0), jnp.concatenate(cand_i, axis=0), PEER_TOPK)
    p = jnp.exp(top_s - top_s[0:1])
    eidx_ref[...] = eidx.astype(jnp.int32)
    gates_ref[...] = p / jnp.sum(p, axis=0, keepdims=True)


def _peer_topk(q, k1, k2, tt):
    n, qw = q.shape
    heads, n_keys, half = k1.shape
    kern = functools.partial(_peer_topk_kernel, half=half)
    out_spec = lambda: pl.BlockSpec((PEER_TOPK, tt), lambda i, h: (h, i))
    return pl.pallas_call(
        kern,
        grid=(n // tt, heads),
        in_specs=[pl.BlockSpec((tt, 2 * half), lambda i, h: (i, h)),
                  pl.BlockSpec((None, n_keys, half), lambda i, h: (h, 0, 0)),
                  pl.BlockSpec((None, n_keys, half), lambda i, h: (h, 0, 0))],
        out_specs=[out_spec(), out_spec()],
        out_shape=[jax.ShapeDtypeStruct((heads * PEER_TOPK, n), jnp.int32),
                   jax.ShapeDtypeStruct((heads * PEER_TOPK, n), F32)],
        compiler_params=pltpu.CompilerParams(dimension_semantics=("parallel", "parallel"),
                                             vmem_limit_bytes=VMEM_LIMIT),
        name="peer_topk",
    )(q, k1, k2)


def _unpack_pair(w):
    lo = pltpu.bitcast(w << 16, F32)
    hi = pltpu.bitcast(w & jnp.uint32(0xFFFF0000), F32)
    return lo, hi


def _spread_matrices(picks):
    pr = lax.broadcasted_iota(jnp.int32, (picks, 2 * picks), 0)
    pc = lax.broadcasted_iota(jnp.int32, (picks, 2 * picks), 1)
    return (jnp.where(pc == 2 * pr, 1.0, 0.0).astype(BF16),
            jnp.where(pc == 2 * pr + 1, 1.0, 0.0).astype(BF16))


def _dense_parts(coef):
    hi, lo = _split_bf16(coef)
    return [jnp.dot(x, s, preferred_element_type=F32)
            for s in _spread_matrices(coef.shape[1]) for x in (hi, lo)]


def _dense_token(rows, part_rows, x1_row):
    b = pltpu.bitcast(rows, BF16)
    sub = lax.broadcasted_iota(jnp.int32, (16, b.shape[0]), 0)
    lhs = jnp.zeros((16, b.shape[0]), F32)
    for k, row in enumerate(part_rows):
        lhs = jnp.where(sub == k, row, lhs)
    r = jnp.dot(lhs.astype(BF16), b, preferred_element_type=F32)
    return jnp.concatenate([r[0:1] + r[1:2], r[2:3] + r[3:4]], axis=1) + x1_row


def _peer_u_kernel(*refs, tg, picks, sub_tokens, dense):
    if dense:
        (idx_ref, tbl_ref, h2_ref, gates_ref, rows_ref, dcoef_ref, dx1_ref,
         coef_ref, dout_ref, act_ref, parts_ref, *y_refs) = refs
    else:
        idx_ref, tbl_ref, h2_ref, gates_ref, coef_ref, act_ref, *y_refs = refs
    s = pl.program_id(1)
    lane = lax.broadcasted_iota(jnp.int32, (1, tg), 1)
    group = len(y_refs)

    def products(t, y_ref):
        hv = h2_ref[t]
        h_lo, h_hi = hv[0:4], hv[4:8]
        for p in range(picks):
            lo, hi = _unpack_pair(tbl_ref[idx_ref[t, p]])
            y_ref[4 * p:4 * p + 4, :] = lo * h_lo + hi * h_hi

    def reduce(t, y_ref):
        part = (y_ref[pl.ds(0, picks, stride=4), :] + y_ref[pl.ds(1, picks, stride=4), :]
                + y_ref[pl.ds(2, picks, stride=4), :] + y_ref[pl.ds(3, picks, stride=4), :])
        act = jnp.sum(part, axis=1, keepdims=True)
        act_ref[...] = jnp.where(lane == t, act, act_ref[...])

    @pl.when(s == 0)
    def _():
        y_refs[-1][...] = jnp.zeros(y_refs[-1].shape, F32)
        act_ref[...] = jnp.zeros(act_ref.shape, F32)

    if dense:
        for k, part in enumerate(_dense_parts(dcoef_ref[...])):
            parts_ref[k] = part

    def body(j, c):
        l0 = group * j
        t0 = s * sub_tokens + l0
        for k in range(group):
            products(t0 + k, y_refs[k])
            reduce(t0 + k - 1, y_refs[k - 1])
            if dense:
                tok = l0 + k
                rows = rows_ref[pl.ds(pl.multiple_of(tok * picks, picks), picks), :]
                part_rows = [parts_ref[q, pl.ds(tok, 1), :] for q in range(4)]
                dout_ref[pl.ds(tok, 1), :] = _dense_token(rows, part_rows, dx1_ref[pl.ds(tok, 1), :])
        return c

    lax.fori_loop(0, sub_tokens // group, body, 0)

    @pl.when(s == tg // sub_tokens - 1)
    def _():
        reduce(tg - 1, y_refs[-1])
        act = act_ref[...]
        gelu = 0.5 * act * (1.0 + lax.erf(act * (2.0 ** -0.5)))
        coef_ref[...] = gates_ref[...] * gelu


def _peer_u(eidx, tbl, h2r, gates, tg, first, n, dense=None):
    picks = eidx.shape[1]
    t0 = first // tg
    sub_tokens = PEER_DENSE_TOKENS if dense else tg
    n_sub = tg // sub_tokens
    kern = functools.partial(_peer_u_kernel, tg=tg, picks=picks, sub_tokens=sub_tokens,
                             dense=dense is not None)
    in_specs = [pl.BlockSpec((tg, picks), lambda i, s: (i + t0, 0), memory_space=pltpu.SMEM,
                             pipeline_mode=pl.Buffered(1)),
                pl.BlockSpec(memory_space=pltpu.VMEM),
                pl.BlockSpec((tg, 8, LANES), lambda i, s: (i + t0, 0, 0)),
                pl.BlockSpec((picks, tg), lambda i, s: (0, i + t0))]
    out_specs = [pl.BlockSpec((picks, tg), lambda i, s: (0, i))]
    out_shape = [jax.ShapeDtypeStruct((picks, n), F32)]
    scratch = [pltpu.VMEM((picks, tg), F32)]
    operands = [eidx, tbl, h2r, gates]
    if dense:
        rows, coef_t, x1, x1_first = dense
        d = x1.shape[1]
        assert coef_t.shape == (n, picks) and rows.shape == (n * picks, d // 2)
        x0 = x1_first // sub_tokens
        sub = lambda w, off: pl.BlockSpec((sub_tokens, w), lambda i, s: (i * n_sub + s + off, 0))
        in_specs += [pl.BlockSpec((sub_tokens * picks, d // 2), lambda i, s: (i * n_sub + s, 0)),
                     sub(picks, 0), sub(d, x0)]
        out_specs.append(sub(d, 0))
        out_shape.append(jax.ShapeDtypeStruct((n, d), F32))
        scratch.append(pltpu.VMEM((4, sub_tokens, 2 * picks), F32))
        operands += [rows, coef_t, x1]
    assert first % tg == 0 and n % tg == 0
    scratch += [pltpu.VMEM((4 * picks, LANES), F32)] * PEER_U_GROUP
    outs = pl.pallas_call(
        kern,
        grid=(n // tg, n_sub),
        in_specs=in_specs,
        out_specs=out_specs,
        out_shape=out_shape,
        scratch_shapes=scratch,
        compiler_params=pltpu.CompilerParams(dimension_semantics=("parallel", "arbitrary"),
                                             vmem_limit_bytes=VMEM_LIMIT),
        name="peer_u_dense" if dense else "peer_u",
    )(*operands)
    return outs if dense else outs[0]


def _sc_gather_rows(table, idx):
    b, w = idx.shape[0], table.shape[1]
    workers = SC_CORES * SC_SUBCORES
    per_w = b // workers
    step = 2 * SC_ROWS
    assert b % workers == 0 and per_w % SC_IDX == 0 and SC_IDX % step == 0
    mesh = plsc.VectorSubcoreMesh(core_axis_name="c", subcore_axis_name="s")

    def body(table_hbm, idx_hbm, out_hbm, idx_v, rows0, rows1, g0, g1, w0, w1):
        wid = lax.axis_index("s") * SC_CORES + lax.axis_index("c")
        base = wid * per_w

        @pl.loop(0, per_w // SC_IDX)
        def _(o):
            off = pl.multiple_of(base + o * SC_IDX, SC_IDX)
            pltpu.sync_copy(idx_hbm.at[pl.ds(off, SC_IDX)], idx_v)

            @pl.loop(0, SC_IDX // step)
            def _(j):
                r0 = pl.multiple_of(j * step, step)
                r1 = pl.multiple_of(r0 + SC_ROWS, SC_ROWS)
                ga = pltpu.async_copy(table_hbm.at[idx_v.at[pl.ds(r0, SC_ROWS)]], rows0, g0)
                gb = pltpu.async_copy(table_hbm.at[idx_v.at[pl.ds(r1, SC_ROWS)]], rows1, g1)
                ga.wait()
                wa = pltpu.async_copy(rows0, out_hbm.at[pl.ds(pl.multiple_of(off + r0, SC_ROWS), SC_ROWS)], w0)
                gb.wait()
                wb = pltpu.async_copy(rows1, out_hbm.at[pl.ds(pl.multiple_of(off + r1, SC_ROWS), SC_ROWS)], w1)
                wa.wait()
                wb.wait()

    return pl.kernel(
        body,
        out_type=jax.ShapeDtypeStruct((b, w), table.dtype),
        mesh=mesh,
        scratch_types=[pltpu.VMEM((SC_IDX,), jnp.int32), pltpu.VMEM((SC_ROWS, w), table.dtype),
                       pltpu.VMEM((SC_ROWS, w), table.dtype)] + [pltpu.SemaphoreType.DMA] * 4,
        name="sc_gather_rows",
    )(table, idx)


def _peer_v_dense_kernel(rows_ref, coef_ref, x1_ref, o_ref, *, tt, picks):
    parts = _dense_parts(coef_ref[...])
    for t in range(tt):
        o_ref[t:t + 1, :] = _dense_token(rows_ref[t * picks:(t + 1) * picks, :],
                                         [part[t:t + 1, :] for part in parts], x1_ref[t:t + 1, :])


def _peer_v_dense(rows, coef_t, x1, x1_first, tt):
    n, picks = coef_t.shape
    d = x1.shape[1]
    x0 = x1_first // tt
    kern = functools.partial(_peer_v_dense_kernel, tt=tt, picks=picks)
    return pl.pallas_call(
        kern,
        grid=(n // tt,),
        in_specs=[pl.BlockSpec((tt * picks, d // 2), lambda i: (i, 0)),
                  pl.BlockSpec((tt, picks), lambda i: (i, 0)),
                  pl.BlockSpec((tt, d), lambda i: (i + x0, 0))],
        out_specs=pl.BlockSpec((tt, d), lambda i: (i, 0)),
        out_shape=jax.ShapeDtypeStruct((n, d), F32),
        compiler_params=pltpu.CompilerParams(dimension_semantics=("parallel",),
                                             vmem_limit_bytes=VMEM_LIMIT),
        name="peer_v_dense",
    )(rows, coef_t, x1)


def _pack_table(t):
    e, d = t.shape
    tb = lax.bitcast_convert_type(t.astype(BF16), jnp.uint16).astype(jnp.uint32)
    packed = tb[:, :d // 2] | (tb[:, d // 2:] << 16)
    return packed.reshape(e, d // (2 * LANES), LANES)


def _tile(n, pref):
    t = min(n, pref)
    assert n % t == 0, (n, pref)
    return t


def kernel(x, mem, attn_norm_g, w_in, conv_w, conv_b, conv_ln_g, conv_ln_b, sb_out_g, mem_norm_g, w_mem_kv, q_norm_g, k_norm_g, mem_out_g, w_out, ffn_norm_g, w_peer_q, peer_keys1, peer_keys2, peer_u, peer_v):
    batch, seq, d = x.shape
    depth = w_in.shape[0]
    conv_ch = conv_w.shape[2]
    sb_width = sb_out_g.shape[1]
    mem_width = mem_out_g.shape[1]
    assert d == 8 * LANES and conv_ch % LANES == 0 and sb_width % LANES == 0
    n = batch * seq
    q_off, k_off, v_off = 2 * conv_ch, 2 * conv_ch + sb_width, 2 * conv_ch + 2 * sb_width
    m_off = 2 * conv_ch + 3 * sb_width
    row = lambda v: v.reshape(1, -1)

    x2 = x.reshape(n, d)
    for l in range(depth):
        proj = _in_proj(x2, row(attn_norm_g[l]), w_in[l].astype(BF16), _tile(n, 512))
        conv_o = _conv(proj, conv_w[l], row(conv_b[l]), row(conv_ln_g[l]), row(conv_ln_b[l]),
                       batch, seq, _tile(seq, 512))
        sb = _stickbreak(proj, batch, seq, _tile(seq, 256), q_off // LANES, k_off // LANES,
                         v_off // LANES, sb_width // LANES)
        heads_m = mem_width // HEAD_DIM
        mk, mv = _mem_kv(mem, row(mem_norm_g[l]), w_mem_kv[l].astype(BF16),
                         row(jnp.tile(k_norm_g[l], heads_m)))
        mem_o = _mem_attn(proj, mk, mv, row(jnp.tile(q_norm_g[l], heads_m)), row(mem_out_g[l]),
                          batch, seq, _tile(seq, 512), m_off // mem_width)
        x1, h2, q = _out_proj(conv_o, sb, mem_o, x2, row(sb_out_g[l]), w_out[l].astype(BF16),
                              row(ffn_norm_g[l]), w_peer_q[l].astype(BF16), _tile(n, 512))
        eidx, gates = _peer_topk(q, peer_keys1[l].astype(BF16), peer_keys2[l].astype(BF16),
                                 _tile(n, 256))
        tg = _tile(n, 128)
        half = n // 2
        eidx_t = eidx.T
        v_tbl = _pack_table(peer_v[l]).reshape(peer_v.shape[1], d // 2)
        u_tbl = _pack_table(peer_u[l])
        h2r = h2.reshape(n, 8, LANES)
        rows = [_sc_gather_rows(v_tbl, eidx_t[k * half:(k + 1) * half].reshape(-1)) for k in range(2)]
        coef0 = _peer_u(eidx_t, u_tbl, h2r, gates, tg, 0, half)
        coef1, out0 = _peer_u(eidx_t, u_tbl, h2r, gates, tg, half, half,
                              dense=(rows[0], coef0.T, x1, 0))
        out1 = _peer_v_dense(rows[1], coef1.T, x1, half, _tile(half, 32))
        x2 = jnp.concatenate([out0, out1], axis=0)
    return x2.reshape(batch, seq, d)
```

```python
import functools

import jax
import jax.numpy as jnp
from jax import lax
from jax.experimental import pallas as pl
from jax.experimental.pallas import tpu as pltpu
from jax.experimental.pallas import tpu_sc as plsc

F32 = jnp.float32
BF16 = jnp.bfloat16
EPS = 1e-6

HEAD_DIM = 64
LANES = 128
CONV_HALO = 32
SB_ROWS = 128
SB_DEAD = -120.0
SC_CORES, SC_SUBCORES = 2, 16
SC_ROWS = 64
SC_IDX = 4096
PEER_U_GROUP = 8
PEER_TOPK = 16
N_KEYS = 128
NEG_INF = float("-inf")
BIG_ID = float(2 ** 24)
VMEM_LIMIT = 56 * 1024 * 1024


def _rms(x, g):
    return x * lax.rsqrt(jnp.mean(x * x, axis=-1, keepdims=True) + EPS) * g


def _dot_nt(a, b):
    return lax.dot_general(a, b, (((1,), (1,)), ((), ())), preferred_element_type=F32)


def _split_bf16(x):
    hi = x.astype(BF16)
    lo = (x - hi.astype(F32)).astype(BF16)
    return hi, lo


def _in_proj_kernel(x_ref, g_ref, w_ref, o_ref):
    h = _rms(x_ref[...], g_ref[...])
    o_ref[...] = jnp.dot(h.astype(BF16), w_ref[...], preferred_element_type=F32).astype(o_ref.dtype)


def _in_proj(x2, g, w, tm):
    n, d = x2.shape
    width = w.shape[1]
    return pl.pallas_call(
        _in_proj_kernel,
        grid=(n // tm,),
        in_specs=[pl.BlockSpec((tm, d), lambda i: (i, 0)),
                  pl.BlockSpec((1, d), lambda i: (0, 0)),
                  pl.BlockSpec((d, width), lambda i: (0, 0))],
        out_specs=pl.BlockSpec((tm, width), lambda i: (i, 0)),
        out_shape=jax.ShapeDtypeStruct((n, width), BF16),
        compiler_params=pltpu.CompilerParams(dimension_semantics=("parallel",),
                                             vmem_limit_bytes=VMEM_LIMIT),
        name="in_proj",
    )(x2, g, w)


def _conv_kernel(a_ref, gate_ref, w_ref, b_ref, lg_ref, lb_ref, o_ref, ubuf, *, tc, width):
    i = pl.program_id(1)

    @pl.when(i == 0)
    def _():
        ubuf[0:CONV_HALO, :] = jnp.zeros((CONV_HALO, ubuf.shape[1]), F32)

    @pl.when(i > 0)
    def _():
        ubuf[0:CONV_HALO, :] = ubuf[tc:tc + CONV_HALO, :]

    a = a_ref[...].astype(F32)
    gate = gate_ref[...].astype(F32)
    ubuf[CONV_HALO:CONV_HALO + tc, :] = a * jax.nn.sigmoid(gate)
    base = CONV_HALO - (width - 1)
    acc = jnp.zeros((tc, ubuf.shape[1]), F32)
    for j in range(width):
        acc = acc + w_ref[j:j + 1, :] * ubuf[base + j:base + j + tc, :]
    y = acc + b_ref[...]
    mu = jnp.mean(y, axis=-1, keepdims=True)
    yc = y - mu
    var = jnp.mean(yc * yc, axis=-1, keepdims=True)
    z = yc * lax.rsqrt(var + EPS) * lg_ref[...] + lb_ref[...]
    o_ref[...] = (z * jax.nn.sigmoid(z)).astype(o_ref.dtype)


def _conv(proj, conv_w, conv_b, ln_g, ln_b, batch, seq, tc):
    n = proj.shape[0]
    width, ch = conv_w.shape
    nt = seq // tc
    kern = functools.partial(_conv_kernel, tc=tc, width=width)
    vec = lambda: pl.BlockSpec((1, ch), lambda b, i: (0, 0))
    return pl.pallas_call(
        kern,
        grid=(batch, nt),
        in_specs=[pl.BlockSpec((tc, ch), lambda b, i: (b * nt + i, 0)),
                  pl.BlockSpec((tc, ch), lambda b, i: (b * nt + i, 1)),
                  pl.BlockSpec((width, ch), lambda b, i: (0, 0)),
                  vec(), vec(), vec()],
        out_specs=pl.BlockSpec((tc, ch), lambda b, i: (b * nt + i, 0)),
        out_shape=jax.ShapeDtypeStruct((n, ch), BF16),
        scratch_shapes=[pltpu.VMEM((tc + CONV_HALO, ch), F32)],
        compiler_params=pltpu.CompilerParams(dimension_semantics=("parallel", "arbitrary"),
                                             vmem_limit_bytes=VMEM_LIMIT),
        name="conv",
    )(proj, proj, conv_w, conv_b, ln_g, ln_b)


def _sb_kernel(q_ref, k_ref, v_ref, o_ref, *scratch, tq):
    n_chain = len(scratch) // 2
    acc_refs, carry_refs = scratch[:n_chain], scratch[n_chain:]
    qi = pl.program_id(2)
    q2 = q_ref[...]
    lane = lax.broadcasted_iota(jnp.int32, (1, LANES), 1)
    row = lax.broadcasted_iota(jnp.int32, (tq, tq), 0)
    col = lax.broadcasted_iota(jnp.int32, (tq, tq), 1)
    suffix = jnp.where(row > col, -1.0, 0.0).astype(BF16)
    scale = jnp.asarray(HEAD_DIM ** -0.5, q2.dtype)
    chains = []
    for h in range(2):
        head = (lane >= h * HEAD_DIM) & (lane < (h + 1) * HEAD_DIM)
        qh = jnp.where(head, q2, jnp.zeros_like(q2)) * scale
        for r0 in range(0, tq, SB_ROWS):
            chains.append((qh[r0:r0 + SB_ROWS], r0))
    for c in range(len(chains)):
        acc_refs[c][...] = jnp.zeros(acc_refs[c].shape, F32)
        carry_refs[c][...] = jnp.zeros(carry_refs[c].shape, F32)

    def stages(kb, diag):
        st = {}
        n = len(chains)

        def qk():
            start = pl.multiple_of(kb * tq, tq)
            st["k"] = k_ref[pl.ds(start, tq), :]
            st["v"] = v_ref[pl.ds(start, tq), :]
            st["z"] = [_dot_nt(qc, st["k"]) for qc, _ in chains]

        def logs():
            st["lg"], st["spb"], st["keep"] = [], [], []
            for z, (_, r0) in zip(st["z"], chains):
                lg = jnp.minimum(z, 0.0) - jnp.log(1.0 + jnp.exp(jnp.minimum(z, -z)))
                sp = z - lg
                if diag:
                    keep = (lax.broadcasted_iota(jnp.int32, (SB_ROWS, tq), 1)
                            < lax.broadcasted_iota(jnp.int32, (SB_ROWS, tq), 0) + r0)
                    sp = jnp.where(keep, sp, 0.0)
                    st["keep"].append(keep)
                st["lg"].append(lg)
                st["spb"].append(sp.astype(BF16))

        def sums():
            st["r"] = [jnp.dot(spb, suffix, preferred_element_type=F32) for spb in st["spb"]]

        def weights():
            st["a"] = []
            for c in range(n):
                after = st["r"][c] + carry_refs[c][...]
                a = jnp.exp(st["lg"][c] + after)
                if diag:
                    a = jnp.where(st["keep"][c], a, 0.0)
                st["a"].append(a.astype(BF16))
                carry_refs[c][...] = after[:, 0:1] - st["spb"][c][:, 0:LANES].astype(F32)[:, 0:1]

        def values():
            for c in range(n):
                acc_refs[c][...] += jnp.dot(st["a"][c], st["v"], preferred_element_type=F32)

        return [qk, logs, sums, weights, values]

    def run(blocks):
        seqs = [stages(kb, diag) for kb, diag in blocks]
        depth = len(seqs[0])
        for step in range(depth + len(seqs) - 1):
            for i, sq in enumerate(seqs):
                if 0 <= step - i < depth:
                    sq[step - i]()

    run([(qi, True)])

    def live(s):
        i, go = s
        return (i < qi // 2) & go

    def pair(s):
        i, _ = s
        run([(qi - 1 - 2 * i, False), (qi - 2 - 2 * i, False)])
        top = functools.reduce(jnp.maximum, [jnp.max(r[...]) for r in carry_refs])
        return i + 1, top > SB_DEAD

    _, go = lax.while_loop(live, pair, (jnp.int32(0), jnp.bool_(True)))

    @pl.when((qi % 2 == 1) & go)
    def _():
        run([(0, False)])

    n_sub = tq // SB_ROWS
    out = [jnp.concatenate([acc_refs[h * n_sub + s][...] for s in range(n_sub)], axis=0) for h in range(2)]
    o_ref[...] = jnp.where(lane < HEAD_DIM, out[0], out[1]).astype(o_ref.dtype)


def _stickbreak(proj, batch, seq, tq, q_col, k_col, v_col, n_pairs):
    n = proj.shape[0]
    nq = seq // tq
    kern = functools.partial(_sb_kernel, tq=tq)
    return pl.pallas_call(
        kern,
        grid=(batch, n_pairs, nq),
        in_specs=[pl.BlockSpec((tq, LANES), lambda b, p, i: (b * nq + i, q_col + p)),
                  pl.BlockSpec((seq, LANES), lambda b, p, i: (b, k_col + p)),
                  pl.BlockSpec((seq, LANES), lambda b, p, i: (b, v_col + p))],
        out_specs=pl.BlockSpec((tq, LANES), lambda b, p, i: (b * nq + i, p)),
        out_shape=jax.ShapeDtypeStruct((n, n_pairs * LANES), F32),
        scratch_shapes=([pltpu.VMEM((SB_ROWS, LANES), F32)] * (2 * tq // SB_ROWS)
                        + [pltpu.VMEM((SB_ROWS, 1), F32)] * (2 * tq // SB_ROWS)),
        compiler_params=pltpu.CompilerParams(
            dimension_semantics=("parallel", "parallel", "arbitrary"),
            vmem_limit_bytes=VMEM_LIMIT),
        name="stickbreak",
    )(proj, proj, proj)


def _head_sumsq(x, blk_ones):
    hi, lo = _split_bf16(x * x)
    return (jnp.dot(hi, blk_ones, preferred_element_type=F32)
            + jnp.dot(lo, blk_ones, preferred_element_type=F32))


def _block_ones(width):
    r = lax.broadcasted_iota(jnp.int32, (width, width), 0) // HEAD_DIM
    c = lax.broadcasted_iota(jnp.int32, (width, width), 1) // HEAD_DIM
    return jnp.where(r == c, 1.0, 0.0).astype(BF16)


def _mem_kv_kernel(mem_ref, g_ref, w_ref, kg_ref, k_ref, v_ref, *, width):
    mn = _rms(mem_ref[...], g_ref[...])
    kv = jnp.dot(mn.astype(BF16), w_ref[...], preferred_element_type=F32)
    k = kv[:, :width]
    ss = _head_sumsq(k, _block_ones(width))
    k_ref[...] = (k * lax.rsqrt(ss * (1.0 / HEAD_DIM) + EPS) * kg_ref[...]).astype(k_ref.dtype)
    v_ref[...] = kv[:, width:].astype(v_ref.dtype)


def _mem_kv(mem, g, w, kg_t):
    b, m, d = mem.shape
    width = w.shape[1] // 2
    kern = functools.partial(_mem_kv_kernel, width=width)
    out = jax.ShapeDtypeStruct((b, m, width), BF16)
    return pl.pallas_call(
        kern,
        grid=(b,),
        in_specs=[pl.BlockSpec((None, m, d), lambda i: (i, 0, 0)),
                  pl.BlockSpec((1, d), lambda i: (0, 0)),
                  pl.BlockSpec((d, 2 * width), lambda i: (0, 0)),
                  pl.BlockSpec((1, width), lambda i: (0, 0))],
        out_specs=[pl.BlockSpec((None, m, width), lambda i: (i, 0, 0)),
                   pl.BlockSpec((None, m, width), lambda i: (i, 0, 0))],
        out_shape=[out, out],
        compiler_params=pltpu.CompilerParams(dimension_semantics=("parallel",),
                                             vmem_limit_bytes=VMEM_LIMIT),
        name="mem_kv",
    )(mem, g, w, kg_t)


def _mem_attn_kernel(q_ref, k_ref, v_ref, qg_ref, og_ref, o_ref, *, width):
    q = q_ref[...].astype(F32)
    ss = _head_sumsq(q, _block_ones(width))
    qn = q * lax.rsqrt(ss * (1.0 / HEAD_DIM) + EPS) * qg_ref[...]
    k = k_ref[...]
    v = v_ref[...]
    lane = lax.broadcasted_iota(jnp.int32, (1, width), 1)
    out = jnp.zeros(q.shape, F32)
    for h in range(width // HEAD_DIM):
        head = (lane >= h * HEAD_DIM) & (lane < (h + 1) * HEAD_DIM)
        qh = jnp.where(head, qn, 0.0).astype(BF16)
        s = _dot_nt(qh, k) * (HEAD_DIM ** -0.5)
        p = jnp.exp(s - jnp.max(s, axis=-1, keepdims=True))
        o = jnp.dot(p.astype(BF16), v, preferred_element_type=F32) / jnp.sum(p, axis=-1, keepdims=True)
        out = jnp.where(head, o, out)
    o_ref[...] = _rms(out, og_ref[...]).astype(o_ref.dtype)


def _mem_attn(proj, k, v, qg_t, og, batch, seq, tm, q_col):
    n = proj.shape[0]
    _, m, width = k.shape
    nt = seq // tm
    kern = functools.partial(_mem_attn_kernel, width=width)
    return pl.pallas_call(
        kern,
        grid=(batch, nt),
        in_specs=[pl.BlockSpec((tm, width), lambda b, i: (b * nt + i, q_col)),
                  pl.BlockSpec((None, m, width), lambda b, i: (b, 0, 0)),
                  pl.BlockSpec((None, m, width), lambda b, i: (b, 0, 0)),
                  pl.BlockSpec((1, width), lambda b, i: (0, 0)),
                  pl.BlockSpec((1, width), lambda b, i: (0, 0))],
        out_specs=pl.BlockSpec((tm, width), lambda b, i: (b * nt + i, 0)),
        out_shape=jax.ShapeDtypeStruct((n, width), BF16),
        compiler_params=pltpu.CompilerParams(dimension_semantics=("parallel", "parallel"),
                                             vmem_limit_bytes=VMEM_LIMIT),
        name="mem_attn",
    )(proj, k, v, qg_t, og)


def _out_proj_kernel(conv_ref, sb_ref, mem_ref, x_ref, sbg_ref, wo_ref, fg_ref, wq_ref,
                     x1_ref, h2_ref, q_ref, *, c_w, s_w):
    sbn = _rms(sb_ref[...], sbg_ref[...]).astype(BF16)
    mixed = jnp.dot(conv_ref[...], wo_ref[0:c_w, :], preferred_element_type=F32)
    mixed = mixed + jnp.dot(sbn, wo_ref[c_w:c_w + s_w, :], preferred_element_type=F32)
    mixed = mixed + jnp.dot(mem_ref[...], wo_ref[c_w + s_w:, :], preferred_element_type=F32)
    x1 = x_ref[...] + mixed
    h2 = _rms(x1, fg_ref[...])
    x1_ref[...] = x1
    h2_ref[...] = h2
    q_ref[...] = jnp.dot(h2.astype(BF16), wq_ref[...], preferred_element_type=F32).astype(q_ref.dtype)


def _out_proj(conv_o, sb, mem_o, x2, sbg, w_out, fg, w_q, tm):
    n, d = x2.shape
    c_w, s_w, m_w = conv_o.shape[1], sb.shape[1], mem_o.shape[1]
    qw = w_q.shape[1]
    kern = functools.partial(_out_proj_kernel, c_w=c_w, s_w=s_w)
    row = lambda w: pl.BlockSpec((tm, w), lambda i: (i, 0))
    full = lambda r, c: pl.BlockSpec((r, c), lambda i: (0, 0))
    return pl.pallas_call(
        kern,
        grid=(n // tm,),
        in_specs=[row(c_w), row(s_w), row(m_w), row(d), full(1, s_w), full(c_w + s_w + m_w, d),
                  full(1, d), full(d, qw)],
        out_specs=[row(d), row(d), row(qw)],
        out_shape=[jax.ShapeDtypeStruct((n, d), F32), jax.ShapeDtypeStruct((n, d), F32),
                   jax.ShapeDtypeStruct((n, qw), BF16)],
        compiler_params=pltpu.CompilerParams(dimension_semantics=("parallel",),
                                             vmem_limit_bytes=VMEM_LIMIT),
        name="out_proj",
    )(conv_o, sb, mem_o, x2, sbg, w_out, fg, w_q)


def _topk_rows(s, ids, k):
    vals, sel = [], []
    for _ in range(k):
        m = jnp.max(s, axis=0, keepdims=True)
        i = jnp.min(jnp.where(s == m, ids, BIG_ID), axis=0, keepdims=True)
        s = jnp.where(ids == i, NEG_INF, s)
        vals.append(m)
        sel.append(i)
    return jnp.concatenate(vals, axis=0), jnp.concatenate(sel, axis=0)


def _peer_topk_kernel(q_ref, k1_ref, k2_ref, eidx_ref, gates_ref, *, half):
    t = q_ref.shape[0]
    key_ids = lax.broadcasted_iota(jnp.int32, (N_KEYS, t), 0).astype(F32)
    q = q_ref[...]
    v1, i1 = _topk_rows(_dot_nt(k1_ref[...], q[:, :half]), key_ids, PEER_TOPK)
    v2, i2 = _topk_rows(_dot_nt(k2_ref[...], q[:, half:]), key_ids, PEER_TOPK)
    cand_s, cand_i = [v1[0:1] + v2], [i1[0:1] * N_KEYS + i2]
    sub = lax.broadcasted_iota(jnp.int32, (8, t), 0)
    for a in range(1, 8):
        ok = sub < (PEER_TOPK // (a + 1))
        cand_s.append(jnp.where(ok, v1[a:a + 1] + v2[0:8], NEG_INF))
        cand_i.append(i1[a:a + 1] * N_KEYS + i2[0:8])
    cand_s.append(v1[8:16] + v2[0:1])
    cand_i.append(i1[8:16] * N_KEYS + i2[0:1])
    top_s, eidx = _topk_rows(jnp.concatenate(cand_s, axis=0), jnp.concatenate(cand_i, axis=0), PEER_TOPK)
    p = jnp.exp(top_s - top_s[0:1])
    eidx_ref[...] = eidx.astype(jnp.int32)
    gates_ref[...] = p / jnp.sum(p, axis=0, keepdims=True)


def _peer_topk(q, k1, k2, tt):
    n, qw = q.shape
    heads, n_keys, half = k1.shape
    kern = functools.partial(_peer_topk_kernel, half=half)
    out_spec = lambda: pl.BlockSpec((PEER_TOPK, tt), lambda i, h: (h, i))
    return pl.pallas_call(
        kern,
        grid=(n // tt, heads),
        in_specs=[pl.BlockSpec((tt, 2 * half), lambda i, h: (i, h)),
                  pl.BlockSpec((None, n_keys, half), lambda i, h: (h, 0, 0)),
                  pl.BlockSpec((None, n_keys, half), lambda i, h: (h, 0, 0))],
        out_specs=[out_spec(), out_spec()],
        out_shape=[jax.ShapeDtypeStruct((heads * PEER_TOPK, n), jnp.int32),
                   jax.ShapeDtypeStruct((heads * PEER_TOPK, n), F32)],
        compiler_params=pltpu.CompilerParams(dimension_semantics=("parallel", "parallel"),
                                             vmem_limit_bytes=VMEM_LIMIT),
        name="peer_topk",
    )(q, k1, k2)


def _unpack_pair(w):
    lo = pltpu.bitcast(w << 16, F32)
    hi = pltpu.bitcast(w & jnp.uint32(0xFFFF0000), F32)
    return lo, hi


def _spread_matrices(picks):
    pr = lax.broadcasted_iota(jnp.int32, (picks, 2 * picks), 0)
    pc = lax.broadcasted_iota(jnp.int32, (picks, 2 * picks), 1)
    return (jnp.where(pc == 2 * pr, 1.0, 0.0).astype(BF16),
            jnp.where(pc == 2 * pr + 1, 1.0, 0.0).astype(BF16))


def _dense_parts(coef):
    hi, lo = _split_bf16(coef)
    return [jnp.dot(x, s, preferred_element_type=F32)
            for s in _spread_matrices(coef.shape[1]) for x in (hi, lo)]


def _dense_token(rows, part_rows, x1_row):
    b = pltpu.bitcast(rows, BF16)
    sub = lax.broadcasted_iota(jnp.int32, (16, b.shape[0]), 0)
    lhs = jnp.zeros((16, b.shape[0]), F32)
    for k, row in enumerate(part_rows):
        lhs = jnp.where(sub == k, row, lhs)
    r = jnp.dot(lhs.astype(BF16), b, preferred_element_type=F32)
    return jnp.concatenate([r[0:1] + r[1:2], r[2:3] + r[3:4]], axis=1) + x1_row


def _peer_u_kernel(idx_ref, tbl_ref, h2_ref, gates_ref, coef_ref, act_ref, *y_refs, tg, picks):
    lane = lax.broadcasted_iota(jnp.int32, (1, tg), 1)
    group = len(y_refs)

    def products(t, y_ref):
        hv = h2_ref[t]
        h_lo, h_hi = hv[0:4], hv[4:8]
        for p in range(picks):
            lo, hi = _unpack_pair(tbl_ref[idx_ref[t, p]])
            y_ref[4 * p:4 * p + 4, :] = lo * h_lo + hi * h_hi

    def reduce(t, y_ref):
        part = (y_ref[pl.ds(0, picks, stride=4), :] + y_ref[pl.ds(1, picks, stride=4), :]
                + y_ref[pl.ds(2, picks, stride=4), :] + y_ref[pl.ds(3, picks, stride=4), :])
        act = jnp.sum(part, axis=1, keepdims=True)
        act_ref[...] = jnp.where(lane == t, act, act_ref[...])

    y_refs[-1][...] = jnp.zeros(y_refs[-1].shape, F32)
    act_ref[...] = jnp.zeros(act_ref.shape, F32)

    def body(j, c):
        t0 = group * j
        for k in range(group):
            products(t0 + k, y_refs[k])
            reduce(t0 + k - 1, y_refs[k - 1])
        return c

    lax.fori_loop(0, tg // group, body, 0)
    reduce(tg - 1, y_refs[-1])
    act = act_ref[...]
    gelu = 0.5 * act * (1.0 + lax.erf(act * (2.0 ** -0.5)))
    coef_ref[...] = gates_ref[...] * gelu


def _peer_u(eidx, tbl, h2r, gates, tg):
    n, picks = eidx.shape
    kern = functools.partial(_peer_u_kernel, tg=tg, picks=picks)
    return pl.pallas_call(
        kern,
        grid=(n // tg,),
        in_specs=[pl.BlockSpec((tg, picks), lambda i: (i, 0), memory_space=pltpu.SMEM,
                               pipeline_mode=pl.Buffered(1)),
                  pl.BlockSpec(memory_space=pltpu.VMEM),
                  pl.BlockSpec((tg, 8, LANES), lambda i: (i, 0, 0)),
                  pl.BlockSpec((picks, tg), lambda i: (0, i))],
        out_specs=pl.BlockSpec((picks, tg), lambda i: (0, i)),
        out_shape=jax.ShapeDtypeStruct((picks, n), F32),
        scratch_shapes=([pltpu.VMEM((picks, tg), F32)]
                        + [pltpu.VMEM((4 * picks, LANES), F32)] * PEER_U_GROUP),
        compiler_params=pltpu.CompilerParams(dimension_semantics=("parallel",),
                                             vmem_limit_bytes=VMEM_LIMIT),
        name="peer_u",
    )(eidx, tbl, h2r, gates)


def _sc_gather_rows(table, idx):
    b, w = idx.shape[0], table.shape[1]
    workers = SC_CORES * SC_SUBCORES
    per_w = b // workers
    step = 2 * SC_ROWS
    assert b % workers == 0 and per_w % SC_IDX == 0 and SC_IDX % step == 0
    mesh = plsc.VectorSubcoreMesh(core_axis_name="c", subcore_axis_name="s")

    def body(table_hbm, idx_hbm, out_hbm, idx_v, rows0, rows1, g0, g1, w0, w1):
        wid = lax.axis_index("s") * SC_CORES + lax.axis_index("c")
        base = wid * per_w

        @pl.loop(0, per_w // SC_IDX)
        def _(o):
            off = pl.multiple_of(base + o * SC_IDX, SC_IDX)
            pltpu.sync_copy(idx_hbm.at[pl.ds(off, SC_IDX)], idx_v)

            @pl.loop(0, SC_IDX // step)
            def _(j):
                r0 = pl.multiple_of(j * step, step)
                r1 = pl.multiple_of(r0 + SC_ROWS, SC_ROWS)
                ga = pltpu.async_copy(table_hbm.at[idx_v.at[pl.ds(r0, SC_ROWS)]], rows0, g0)
                gb = pltpu.async_copy(table_hbm.at[idx_v.at[pl.ds(r1, SC_ROWS)]], rows1, g1)
                ga.wait()
                wa = pltpu.async_copy(rows0, out_hbm.at[pl.ds(pl.multiple_of(off + r0, SC_ROWS), SC_ROWS)], w0)
                gb.wait()
                wb = pltpu.async_copy(rows1, out_hbm.at[pl.ds(pl.multiple_of(off + r1, SC_ROWS), SC_ROWS)], w1)
                wa.wait()
                wb.wait()

    return pl.kernel(
        body,
        out_type=jax.ShapeDtypeStruct((b, w), table.dtype),
        mesh=mesh,
        scratch_types=[pltpu.VMEM((SC_IDX,), jnp.int32), pltpu.VMEM((SC_ROWS, w), table.dtype),
                       pltpu.VMEM((SC_ROWS, w), table.dtype)] + [pltpu.SemaphoreType.DMA] * 4,
        name="sc_gather_rows",
    )(table, idx)


def _peer_v_dense_kernel(rows_ref, coef_ref, x1_ref, o_ref, *, tt, picks):
    parts = _dense_parts(coef_ref[...])
    for t in range(tt):
        o_ref[t:t + 1, :] = _dense_token(rows_ref[t * picks:(t + 1) * picks, :],
                                         [part[t:t + 1, :] for part in parts], x1_ref[t:t + 1, :])


def _peer_v_dense(rows, coef_t, x1, tt):
    n, picks = coef_t.shape
    d = x1.shape[1]
    kern = functools.partial(_peer_v_dense_kernel, tt=tt, picks=picks)
    return pl.pallas_call(
        kern,
        grid=(n // tt,),
        in_specs=[pl.BlockSpec((tt * picks, d // 2), lambda i: (i, 0)),
                  pl.BlockSpec((tt, picks), lambda i: (i, 0)),
                  pl.BlockSpec((tt, d), lambda i: (i, 0))],
        out_specs=pl.BlockSpec((tt, d), lambda i: (i, 0)),
        out_shape=jax.ShapeDtypeStruct((n, d), F32),
        compiler_params=pltpu.CompilerParams(dimension_semantics=("parallel",),
                                             vmem_limit_bytes=VMEM_LIMIT),
        name="peer_v_dense",
    )(rows, coef_t, x1)


def _pack_table(t):
    e, d = t.shape
    tb = lax.bitcast_convert_type(t.astype(BF16), jnp.uint16).astype(jnp.uint32)
    packed = tb[:, :d // 2] | (tb[:, d // 2:] << 16)
    return packed.reshape(e, d // (2 * LANES), LANES)


def _tile(n, pref):
    t = min(n, pref)
    assert n % t == 0, (n, pref)
    return t


def kernel(x, mem, attn_norm_g, w_in, conv_w, conv_b, conv_ln_g, conv_ln_b, sb_out_g, mem_norm_g, w_mem_kv, q_norm_g, k_norm_g, mem_out_g, w_out, ffn_norm_g, w_peer_q, peer_keys1, peer_keys2, peer_u, peer_v):
    batch, seq, d = x.shape
    depth = w_in.shape[0]
    conv_ch = conv_w.shape[2]
    sb_width = sb_out_g.shape[1]
    mem_width = mem_out_g.shape[1]
    assert d == 8 * LANES and conv_ch % LANES == 0 and sb_width % LANES == 0
    n = batch * seq
    q_off, k_off, v_off = 2 * conv_ch, 2 * conv_ch + sb_width, 2 * conv_ch + 2 * sb_width
    m_off = 2 * conv_ch + 3 * sb_width
    row = lambda v: v.reshape(1, -1)

    x2 = x.reshape(n, d)
    for l in range(depth):
        proj = _in_proj(x2, row(attn_norm_g[l]), w_in[l].astype(BF16), _tile(n, 512))
        conv_o = _conv(proj, conv_w[l], row(conv_b[l]), row(conv_ln_g[l]), row(conv_ln_b[l]),
                       batch, seq, _tile(seq, 512))
        sb = _stickbreak(proj, batch, seq, _tile(seq, 256), q_off // LANES, k_off // LANES,
                         v_off // LANES, sb_width // LANES)
        heads_m = mem_width // HEAD_DIM
        mk, mv = _mem_kv(mem, row(mem_norm_g[l]), w_mem_kv[l].astype(BF16),
                         row(jnp.tile(k_norm_g[l], heads_m)))
        mem_o = _mem_attn(proj, mk, mv, row(jnp.tile(q_norm_g[l], heads_m)), row(mem_out_g[l]),
                          batch, seq, _tile(seq, 512), m_off // mem_width)
        x1, h2, q = _out_proj(conv_o, sb, mem_o, x2, row(sb_out_g[l]), w_out[l].astype(BF16),
                              row(ffn_norm_g[l]), w_peer_q[l].astype(BF16), _tile(n, 512))
        eidx, gates = _peer_topk(q, peer_keys1[l].astype(BF16), peer_keys2[l].astype(BF16),
                                 _tile(n, 512))
        tg = _tile(n, 128)
        eidx_t = eidx.T
        v_rows = _sc_gather_rows(_pack_table(peer_v[l]).reshape(peer_v.shape[1], d // 2),
                                 eidx_t.reshape(-1))
        coef = _peer_u(eidx_t, _pack_table(peer_u[l]), h2.reshape(n, 8, LANES), gates, tg)
        x2 = _peer_v_dense(v_rows, coef.T, x1, _tile(n, 32))
    return x2.reshape(batch, seq, d)
```

```python
import functools

import jax
import jax.numpy as jnp
from jax import lax
from jax.experimental import pallas as pl
from jax.experimental.pallas import tpu as pltpu
from jax.experimental.pallas import tpu_sc as plsc

F32 = jnp.float32
BF16 = jnp.bfloat16
EPS = 1e-6

HEAD_DIM = 64
LANES = 128
SUBLANES = 8
BF16_ROWS = 16
ROW_WORDS = 4
CONV_HALO = 32
SB_ROWS = 128
SB_DEAD = -120.0
SC_CORES, SC_SUBCORES = 2, 16
SC_ROWS = 64
SC_IDX = 4096
PEER_U_GROUP = 8
PEER_TOPK = 16
N_KEYS = 128
NEG_INF = float("-inf")
BIG_ID = float(2 ** 24)
VMEM_LIMIT = 56 * 1024 * 1024


def _rms(x, g):
    return x * lax.rsqrt(jnp.mean(x * x, axis=-1, keepdims=True) + EPS) * g


def _dot_nt(a, b):
    return lax.dot_general(a, b, (((1,), (1,)), ((), ())), preferred_element_type=F32)


def _split_bf16(x):
    hi = x.astype(BF16)
    lo = (x - hi.astype(F32)).astype(BF16)
    return hi, lo


def _in_proj_kernel(x_ref, g_ref, w_ref, o_ref):
    h = _rms(x_ref[...], g_ref[...])
    o_ref[...] = jnp.dot(h.astype(BF16), w_ref[...], preferred_element_type=F32).astype(o_ref.dtype)


def _in_proj(x2, g, w, tm):
    n, d = x2.shape
    width = w.shape[1]
    return pl.pallas_call(
        _in_proj_kernel,
        grid=(n // tm,),
        in_specs=[pl.BlockSpec((tm, d), lambda i: (i, 0)),
                  pl.BlockSpec((1, d), lambda i: (0, 0)),
                  pl.BlockSpec((d, width), lambda i: (0, 0))],
        out_specs=pl.BlockSpec((tm, width), lambda i: (i, 0)),
        out_shape=jax.ShapeDtypeStruct((n, width), BF16),
        compiler_params=pltpu.CompilerParams(dimension_semantics=("parallel",),
                                             vmem_limit_bytes=VMEM_LIMIT),
        name="in_proj",
    )(x2, g, w)


def _conv_kernel(a_ref, gate_ref, w_ref, b_ref, lg_ref, lb_ref, o_ref, ubuf, *, tc, width):
    i = pl.program_id(1)

    @pl.when(i == 0)
    def _():
        ubuf[0:CONV_HALO, :] = jnp.zeros((CONV_HALO, ubuf.shape[1]), F32)

    @pl.when(i > 0)
    def _():
        ubuf[0:CONV_HALO, :] = ubuf[tc:tc + CONV_HALO, :]

    a = a_ref[...].astype(F32)
    gate = gate_ref[...].astype(F32)
    ubuf[CONV_HALO:CONV_HALO + tc, :] = a * jax.nn.sigmoid(gate)
    base = CONV_HALO - (width - 1)
    acc = jnp.zeros((tc, ubuf.shape[1]), F32)
    for j in range(width):
        acc = acc + w_ref[j:j + 1, :] * ubuf[base + j:base + j + tc, :]
    y = acc + b_ref[...]
    mu = jnp.mean(y, axis=-1, keepdims=True)
    yc = y - mu
    var = jnp.mean(yc * yc, axis=-1, keepdims=True)
    z = yc * lax.rsqrt(var + EPS) * lg_ref[...] + lb_ref[...]
    o_ref[...] = (z * jax.nn.sigmoid(z)).astype(o_ref.dtype)


def _conv(proj, conv_w, conv_b, ln_g, ln_b, batch, seq, tc):
    n = proj.shape[0]
    width, ch = conv_w.shape
    nt = seq // tc
    kern = functools.partial(_conv_kernel, tc=tc, width=width)
    vec = lambda: pl.BlockSpec((1, ch), lambda b, i: (0, 0))
    return pl.pallas_call(
        kern,
        grid=(batch, nt),
        in_specs=[pl.BlockSpec((tc, ch), lambda b, i: (b * nt + i, 0)),
                  pl.BlockSpec((tc, ch), lambda b, i: (b * nt + i, 1)),
                  pl.BlockSpec((width, ch), lambda b, i: (0, 0)),
                  vec(), vec(), vec()],
        out_specs=pl.BlockSpec((tc, ch), lambda b, i: (b * nt + i, 0)),
        out_shape=jax.ShapeDtypeStruct((n, ch), BF16),
        scratch_shapes=[pltpu.VMEM((tc + CONV_HALO, ch), F32)],
        compiler_params=pltpu.CompilerParams(dimension_semantics=("parallel", "arbitrary"),
                                             vmem_limit_bytes=VMEM_LIMIT),
        name="conv",
    )(proj, proj, conv_w, conv_b, ln_g, ln_b)


def _sb_kernel(q_ref, k_ref, v_ref, o_ref, *scratch, tq):
    n_chain = len(scratch) // 2
    acc_refs, carry_refs = scratch[:n_chain], scratch[n_chain:]
    qi = pl.program_id(2)
    q2 = q_ref[...]
    lane = lax.broadcasted_iota(jnp.int32, (1, LANES), 1)
    row = lax.broadcasted_iota(jnp.int32, (tq, tq), 0)
    col = lax.broadcasted_iota(jnp.int32, (tq, tq), 1)
    suffix = jnp.where(row > col, -1.0, 0.0).astype(BF16)
    scale = jnp.asarray(HEAD_DIM ** -0.5, q2.dtype)
    chains = []
    for h in range(2):
        head = (lane >= h * HEAD_DIM) & (lane < (h + 1) * HEAD_DIM)
        qh = jnp.where(head, q2, jnp.zeros_like(q2)) * scale
        for r0 in range(0, tq, SB_ROWS):
            chains.append((qh[r0:r0 + SB_ROWS], r0))
    for c in range(len(chains)):
        acc_refs[c][...] = jnp.zeros(acc_refs[c].shape, F32)
        carry_refs[c][...] = jnp.zeros(carry_refs[c].shape, F32)

    def stages(kb, diag):
        st = {}
        n = len(chains)

        def qk():
            start = pl.multiple_of(kb * tq, tq)
            st["k"] = k_ref[pl.ds(start, tq), :]
            st["v"] = v_ref[pl.ds(start, tq), :]
            st["z"] = [_dot_nt(qc, st["k"]) for qc, _ in chains]

        def logs():
            st["lg"], st["spb"], st["keep"] = [], [], []
            for z, (_, r0) in zip(st["z"], chains):
                lg = jnp.minimum(z, 0.0) - jnp.log(1.0 + jnp.exp(jnp.minimum(z, -z)))
                sp = z - lg
                if diag:
                    keep = (lax.broadcasted_iota(jnp.int32, (SB_ROWS, tq), 1)
                            < lax.broadcasted_iota(jnp.int32, (SB_ROWS, tq), 0) + r0)
                    sp = jnp.where(keep, sp, 0.0)
                    st["keep"].append(keep)
                st["lg"].append(lg)
                st["spb"].append(sp.astype(BF16))

        def sums():
            st["r"] = [jnp.dot(spb, suffix, preferred_element_type=F32) for spb in st["spb"]]

        def weights():
            st["a"] = []
            for c in range(n):
                after = st["r"][c] + carry_refs[c][...]
                a = jnp.exp(st["lg"][c] + after)
                if diag:
                    a = jnp.where(st["keep"][c], a, 0.0)
                st["a"].append(a.astype(BF16))
                carry_refs[c][...] = after[:, 0:1] - st["spb"][c][:, 0:LANES].astype(F32)[:, 0:1]

        def values():
            for c in range(n):
                acc_refs[c][...] += jnp.dot(st["a"][c], st["v"], preferred_element_type=F32)

        return [qk, logs, sums, weights, values]

    def run(blocks):
        seqs = [stages(kb, diag) for kb, diag in blocks]
        depth = len(seqs[0])
        for step in range(depth + len(seqs) - 1):
            for i, sq in enumerate(seqs):
                if 0 <= step - i < depth:
                    sq[step - i]()

    run([(qi, True)])

    def alive():
        top = functools.reduce(jnp.maximum, [jnp.max(r[...]) for r in carry_refs])
        return top > SB_DEAD

    @pl.when(qi >= 1)
    def _():
        run([(qi - 1, False)])

    rest = jnp.maximum(qi - 1, 0)

    def live(s):
        i, go = s
        return (i < rest // 2) & go

    def pair(s):
        i, _ = s
        run([(qi - 2 - 2 * i, False), (qi - 3 - 2 * i, False)])
        return i + 1, alive()

    _, go = lax.while_loop(live, pair, (jnp.int32(0), alive()))

    @pl.when((rest % 2 == 1) & go)
    def _():
        run([(0, False)])

    n_sub = tq // SB_ROWS
    out = [jnp.concatenate([acc_refs[h * n_sub + s][...] for s in range(n_sub)], axis=0) for h in range(2)]
    o_ref[...] = jnp.where(lane < HEAD_DIM, out[0], out[1]).astype(o_ref.dtype)


def _stickbreak(proj, batch, seq, tq, q_col, k_col, v_col, n_pairs):
    n = proj.shape[0]
    nq = seq // tq
    kern = functools.partial(_sb_kernel, tq=tq)
    return pl.pallas_call(
        kern,
        grid=(batch, n_pairs, nq),
        in_specs=[pl.BlockSpec((tq, LANES), lambda b, p, i: (b * nq + i, q_col + p)),
                  pl.BlockSpec((seq, LANES), lambda b, p, i: (b, k_col + p)),
                  pl.BlockSpec((seq, LANES), lambda b, p, i: (b, v_col + p))],
        out_specs=pl.BlockSpec((tq, LANES), lambda b, p, i: (b * nq + i, p)),
        out_shape=jax.ShapeDtypeStruct((n, n_pairs * LANES), F32),
        scratch_shapes=([pltpu.VMEM((SB_ROWS, LANES), F32)] * (2 * tq // SB_ROWS)
                        + [pltpu.VMEM((SB_ROWS, 1), F32)] * (2 * tq // SB_ROWS)),
        compiler_params=pltpu.CompilerParams(
            dimension_semantics=("parallel", "parallel", "arbitrary"),
            vmem_limit_bytes=VMEM_LIMIT),
        name="stickbreak",
    )(proj, proj, proj)


def _head_sumsq(x, blk_ones):
    hi, lo = _split_bf16(x * x)
    return (jnp.dot(hi, blk_ones, preferred_element_type=F32)
            + jnp.dot(lo, blk_ones, preferred_element_type=F32))


def _block_ones(width):
    r = lax.broadcasted_iota(jnp.int32, (width, width), 0) // HEAD_DIM
    c = lax.broadcasted_iota(jnp.int32, (width, width), 1) // HEAD_DIM
    return jnp.where(r == c, 1.0, 0.0).astype(BF16)


def _mem_kv_kernel(mem_ref, g_ref, w_ref, kg_ref, k_ref, v_ref, *, width):
    mn = _rms(mem_ref[...], g_ref[...])
    kv = jnp.dot(mn.astype(BF16), w_ref[...], preferred_element_type=F32)
    k = kv[:, :width]
    ss = _head_sumsq(k, _block_ones(width))
    k_ref[...] = (k * lax.rsqrt(ss * (1.0 / HEAD_DIM) + EPS) * kg_ref[...]).astype(k_ref.dtype)
    v_ref[...] = kv[:, width:].astype(v_ref.dtype)


def _mem_kv(mem, g, w, kg_t):
    b, m, d = mem.shape
    width = w.shape[1] // 2
    kern = functools.partial(_mem_kv_kernel, width=width)
    out = jax.ShapeDtypeStruct((b, m, width), BF16)
    return pl.pallas_call(
        kern,
        grid=(b,),
        in_specs=[pl.BlockSpec((None, m, d), lambda i: (i, 0, 0)),
                  pl.BlockSpec((1, d), lambda i: (0, 0)),
                  pl.BlockSpec((d, 2 * width), lambda i: (0, 0)),
                  pl.BlockSpec((1, width), lambda i: (0, 0))],
        out_specs=[pl.BlockSpec((None, m, width), lambda i: (i, 0, 0)),
                   pl.BlockSpec((None, m, width), lambda i: (i, 0, 0))],
        out_shape=[out, out],
        compiler_params=pltpu.CompilerParams(dimension_semantics=("parallel",),
                                             vmem_limit_bytes=VMEM_LIMIT),
        name="mem_kv",
    )(mem, g, w, kg_t)


def _mem_attn_kernel(q_ref, k_ref, v_ref, qg_ref, og_ref, o_ref, *, width):
    q = q_ref[...].astype(F32)
    ss = _head_sumsq(q, _block_ones(width))
    qn = q * lax.rsqrt(ss * (1.0 / HEAD_DIM) + EPS) * qg_ref[...]
    k = k_ref[...]
    v = v_ref[...]
    lane = lax.broadcasted_iota(jnp.int32, (1, width), 1)
    out = jnp.zeros(q.shape, F32)
    for h in range(width // HEAD_DIM):
        head = (lane >= h * HEAD_DIM) & (lane < (h + 1) * HEAD_DIM)
        qh = jnp.where(head, qn, 0.0).astype(BF16)
        s = _dot_nt(qh, k) * (HEAD_DIM ** -0.5)
        p = jnp.exp(s - jnp.max(s, axis=-1, keepdims=True))
        o = jnp.dot(p.astype(BF16), v, preferred_element_type=F32) / jnp.sum(p, axis=-1, keepdims=True)
        out = jnp.where(head, o, out)
    o_ref[...] = _rms(out, og_ref[...]).astype(o_ref.dtype)


def _mem_attn(proj, k, v, qg_t, og, batch, seq, tm, q_col):
    n = proj.shape[0]
    _, m, width = k.shape
    nt = seq // tm
    kern = functools.partial(_mem_attn_kernel, width=width)
    return pl.pallas_call(
        kern,
        grid=(batch, nt),
        in_specs=[pl.BlockSpec((tm, width), lambda b, i: (b * nt + i, q_col)),
                  pl.BlockSpec((None, m, width), lambda b, i: (b, 0, 0)),
                  pl.BlockSpec((None, m, width), lambda b, i: (b, 0, 0)),
                  pl.BlockSpec((1, width), lambda b, i: (0, 0)),
                  pl.BlockSpec((1, width), lambda b, i: (0, 0))],
        out_specs=pl.BlockSpec((tm, width), lambda b, i: (b * nt + i, 0)),
        out_shape=jax.ShapeDtypeStruct((n, width), BF16),
        compiler_params=pltpu.CompilerParams(dimension_semantics=("parallel", "parallel"),
                                             vmem_limit_bytes=VMEM_LIMIT),
        name="mem_attn",
    )(proj, k, v, qg_t, og)


def _out_proj_kernel(conv_ref, sb_ref, mem_ref, x_ref, sbg_ref, wo_ref, fg_ref, wq_ref,
                     x1_ref, h2_ref, q_ref, *, c_w, s_w):
    sbn = _rms(sb_ref[...], sbg_ref[...]).astype(BF16)
    mixed = jnp.dot(conv_ref[...], wo_ref[0:c_w, :], preferred_element_type=F32)
    mixed = mixed + jnp.dot(sbn, wo_ref[c_w:c_w + s_w, :], preferred_element_type=F32)
    mixed = mixed + jnp.dot(mem_ref[...], wo_ref[c_w + s_w:, :], preferred_element_type=F32)
    x1 = x_ref[...] + mixed
    h2 = _rms(x1, fg_ref[...])
    x1_ref[...] = x1
    h2_ref[...] = h2
    q_ref[...] = jnp.dot(h2.astype(BF16), wq_ref[...], preferred_element_type=F32).astype(q_ref.dtype)


def _out_proj(conv_o, sb, mem_o, x2, sbg, w_out, fg, w_q, tm):
    n, d = x2.shape
    c_w, s_w, m_w = conv_o.shape[1], sb.shape[1], mem_o.shape[1]
    qw = w_q.shape[1]
    kern = functools.partial(_out_proj_kernel, c_w=c_w, s_w=s_w)
    row = lambda w: pl.BlockSpec((tm, w), lambda i: (i, 0))
    full = lambda r, c: pl.BlockSpec((r, c), lambda i: (0, 0))
    return pl.pallas_call(
        kern,
        grid=(n // tm,),
        in_specs=[row(c_w), row(s_w), row(m_w), row(d), full(1, s_w), full(c_w + s_w + m_w, d),
                  full(1, d), full(d, qw)],
        out_specs=[row(d), row(d), row(qw)],
        out_shape=[jax.ShapeDtypeStruct((n, d), F32), jax.ShapeDtypeStruct((n, d), F32),
                   jax.ShapeDtypeStruct((n, qw), BF16)],
        compiler_params=pltpu.CompilerParams(dimension_semantics=("parallel",),
                                             vmem_limit_bytes=VMEM_LIMIT),
        name="out_proj",
    )(conv_o, sb, mem_o, x2, sbg, w_out, fg, w_q)


def _topk_rows(s, ids, k):
    vals, sel = [], []
    for _ in range(k):
        m = jnp.max(s, axis=0, keepdims=True)
        i = jnp.min(jnp.where(s == m, ids, BIG_ID), axis=0, keepdims=True)
        s = jnp.where(ids == i, NEG_INF, s)
        vals.append(m)
        sel.append(i)
    return jnp.concatenate(vals, axis=0), jnp.concatenate(sel, axis=0)


def _peer_topk_kernel(q_ref, k1_ref, k2_ref, eidx_ref, gates_ref, *, half):
    t = q_ref.shape[0]
    key_ids = lax.broadcasted_iota(jnp.int32, (N_KEYS, t), 0).astype(F32)
    q = q_ref[...]
    v1, i1 = _topk_rows(_dot_nt(k1_ref[...], q[:, :half]), key_ids, PEER_TOPK)
    v2, i2 = _topk_rows(_dot_nt(k2_ref[...], q[:, half:]), key_ids, PEER_TOPK)
    cand_s, cand_i = [v1[0:1] + v2], [i1[0:1] * N_KEYS + i2]
    sub = lax.broadcasted_iota(jnp.int32, (SUBLANES, t), 0)
    for a in range(1, SUBLANES):
        ok = sub < (PEER_TOPK // (a + 1))
        cand_s.append(jnp.where(ok, v1[a:a + 1] + v2[0:SUBLANES], NEG_INF))
        cand_i.append(i1[a:a + 1] * N_KEYS + i2[0:SUBLANES])
    cand_s.append(v1[SUBLANES:] + v2[0:1])
    cand_i.append(i1[SUBLANES:] * N_KEYS + i2[0:1])
    top_s, eidx = _topk_rows(jnp.concatenate(cand_s, axis=0), jnp.concatenate(cand_i, axis=0), PEER_TOPK)
    p = jnp.exp(top_s - top_s[0:1])
    eidx_ref[...] = eidx.astype(jnp.int32)
    gates_ref[...] = p / jnp.sum(p, axis=0, keepdims=True)


def _peer_topk(q, k1, k2, tt):
    n, qw = q.shape
    heads, n_keys, half = k1.shape
    kern = functools.partial(_peer_topk_kernel, half=half)
    out_spec = lambda: pl.BlockSpec((PEER_TOPK, tt), lambda i, h: (h, i))
    return pl.pallas_call(
        kern,
        grid=(n // tt, heads),
        in_specs=[pl.BlockSpec((tt, 2 * half), lambda i, h: (i, h)),
                  pl.BlockSpec((None, n_keys, half), lambda i, h: (h, 0, 0)),
                  pl.BlockSpec((None, n_keys, half), lambda i, h: (h, 0, 0))],
        out_specs=[out_spec(), out_spec()],
        out_shape=[jax.ShapeDtypeStruct((heads * PEER_TOPK, n), jnp.int32),
                   jax.ShapeDtypeStruct((heads * PEER_TOPK, n), F32)],
        compiler_params=pltpu.CompilerParams(dimension_semantics=("parallel", "parallel"),
                                             vmem_limit_bytes=VMEM_LIMIT),
        name="peer_topk",
    )(q, k1, k2)


def _unpack_pair(w):
    lo = pltpu.bitcast(w << 16, F32)
    hi = pltpu.bitcast(w & jnp.uint32(0xFFFF0000), F32)
    return lo, hi


def _spread_matrices(picks):
    pr = lax.broadcasted_iota(jnp.int32, (picks, 2 * picks), 0)
    pc = lax.broadcasted_iota(jnp.int32, (picks, 2 * picks), 1)
    return (jnp.where(pc == 2 * pr, 1.0, 0.0).astype(BF16),
            jnp.where(pc == 2 * pr + 1, 1.0, 0.0).astype(BF16))


def _dense_parts(coef):
    hi, lo = _split_bf16(coef)
    return [jnp.dot(x, s, preferred_element_type=F32)
            for s in _spread_matrices(coef.shape[1]) for x in (hi, lo)]


def _dense_token(rows, part_rows, x1_row):
    b = pltpu.bitcast(rows, BF16)
    sub = lax.broadcasted_iota(jnp.int32, (BF16_ROWS, b.shape[0]), 0)
    lhs = jnp.zeros((BF16_ROWS, b.shape[0]), F32)
    for k, row in enumerate(part_rows):
        lhs = jnp.where(sub == k, row, lhs)
    r = jnp.dot(lhs.astype(BF16), b, preferred_element_type=F32)
    return jnp.concatenate([r[0:1] + r[1:2], r[2:3] + r[3:4]], axis=1) + x1_row


def _peer_u_kernel(idx_ref, tbl_ref, h2_ref, gates_ref, coef_ref, act_ref, *y_refs, tg, picks):
    lane = lax.broadcasted_iota(jnp.int32, (1, tg), 1)
    group = len(y_refs)

    def products(t, y_ref):
        hv = h2_ref[t]
        h_lo, h_hi = hv[0:ROW_WORDS], hv[ROW_WORDS:]
        for p in range(picks):
            lo, hi = _unpack_pair(tbl_ref[idx_ref[t, p]])
            y_ref[ROW_WORDS * p:ROW_WORDS * (p + 1), :] = lo * h_lo + hi * h_hi

    def reduce(t, y_ref):
        part = functools.reduce(jnp.add, [y_ref[pl.ds(r, picks, stride=ROW_WORDS), :]
                                          for r in range(ROW_WORDS)])
        act = jnp.sum(part, axis=1, keepdims=True)
        act_ref[...] = jnp.where(lane == t, act, act_ref[...])

    y_refs[-1][...] = jnp.zeros(y_refs[-1].shape, F32)
    act_ref[...] = jnp.zeros(act_ref.shape, F32)

    def body(j, c):
        t0 = group * j
        for k in range(group):
            products(t0 + k, y_refs[k])
            reduce(t0 + k - 1, y_refs[k - 1])
        return c

    lax.fori_loop(0, tg // group, body, 0)
    reduce(tg - 1, y_refs[-1])
    act = act_ref[...]
    gelu = 0.5 * act * (1.0 + lax.erf(act * (2.0 ** -0.5)))
    coef_ref[...] = gates_ref[...] * gelu


def _peer_u(eidx, tbl, h2r, gates, tg):
    n, picks = eidx.shape
    kern = functools.partial(_peer_u_kernel, tg=tg, picks=picks)
    return pl.pallas_call(
        kern,
        grid=(n // tg,),
        in_specs=[pl.BlockSpec((tg, picks), lambda i: (i, 0), memory_space=pltpu.SMEM,
                               pipeline_mode=pl.Buffered(1)),
                  pl.BlockSpec(memory_space=pltpu.VMEM),
                  pl.BlockSpec((tg, SUBLANES, LANES), lambda i: (i, 0, 0)),
                  pl.BlockSpec((picks, tg), lambda i: (0, i))],
        out_specs=pl.BlockSpec((picks, tg), lambda i: (0, i)),
        out_shape=jax.ShapeDtypeStruct((picks, n), F32),
        scratch_shapes=([pltpu.VMEM((picks, tg), F32)]
                        + [pltpu.VMEM((ROW_WORDS * picks, LANES), F32)] * PEER_U_GROUP),
        compiler_params=pltpu.CompilerParams(dimension_semantics=("parallel",),
                                             vmem_limit_bytes=VMEM_LIMIT),
        name="peer_u",
    )(eidx, tbl, h2r, gates)


def _sc_gather_rows(table, idx):
    b, w = idx.shape[0], table.shape[1]
    workers = SC_CORES * SC_SUBCORES
    per_w = b // workers
    step = 2 * SC_ROWS
    assert b % workers == 0 and per_w % SC_IDX == 0 and SC_IDX % step == 0
    mesh = plsc.VectorSubcoreMesh(core_axis_name="c", subcore_axis_name="s")

    def body(table_hbm, idx_hbm, out_hbm, idx_v, rows0, rows1, g0, g1, w0, w1):
        wid = lax.axis_index("s") * SC_CORES + lax.axis_index("c")
        base = wid * per_w

        @pl.loop(0, per_w // SC_IDX)
        def _(o):
            off = pl.multiple_of(base + o * SC_IDX, SC_IDX)
            pltpu.sync_copy(idx_hbm.at[pl.ds(off, SC_IDX)], idx_v)

            @pl.loop(0, SC_IDX // step)
            def _(j):
                r0 = pl.multiple_of(j * step, step)
                r1 = pl.multiple_of(r0 + SC_ROWS, SC_ROWS)
                ga = pltpu.async_copy(table_hbm.at[idx_v.at[pl.ds(r0, SC_ROWS)]], rows0, g0)
                gb = pltpu.async_copy(table_hbm.at[idx_v.at[pl.ds(r1, SC_ROWS)]], rows1, g1)
                ga.wait()
                wa = pltpu.async_copy(rows0, out_hbm.at[pl.ds(pl.multiple_of(off + r0, SC_ROWS), SC_ROWS)], w0)
                gb.wait()
                wb = pltpu.async_copy(rows1, out_hbm.at[pl.ds(pl.multiple_of(off + r1, SC_ROWS), SC_ROWS)], w1)
                wa.wait()
                wb.wait()

    return pl.kernel(
        body,
        out_type=jax.ShapeDtypeStruct((b, w), table.dtype),
        mesh=mesh,
        scratch_types=[pltpu.VMEM((SC_IDX,), jnp.int32), pltpu.VMEM((SC_ROWS, w), table.dtype),
                       pltpu.VMEM((SC_ROWS, w), table.dtype)] + [pltpu.SemaphoreType.DMA] * 4,
        name="sc_gather_rows",
    )(table, idx)


def _peer_v_dense_kernel(rows_ref, coef_ref, x1_ref, o_ref, *, tt, picks):
    parts = _dense_parts(coef_ref[...])
    for t in range(tt):
        o_ref[t:t + 1, :] = _dense_token(rows_ref[t * picks:(t + 1) * picks, :],
                                         [part[t:t + 1, :] for part in parts], x1_ref[t:t + 1, :])


def _peer_v_dense(rows, coef_t, x1, tt):
    n, picks = coef_t.shape
    d = x1.shape[1]
    kern = functools.partial(_peer_v_dense_kernel, tt=tt, picks=picks)
    return pl.pallas_call(
        kern,
        grid=(n // tt,),
        in_specs=[pl.BlockSpec((tt * picks, d // 2), lambda i: (i, 0)),
                  pl.BlockSpec((tt, picks), lambda i: (i, 0)),
                  pl.BlockSpec((tt, d), lambda i: (i, 0))],
        out_specs=pl.BlockSpec((tt, d), lambda i: (i, 0)),
        out_shape=jax.ShapeDtypeStruct((n, d), F32),
        compiler_params=pltpu.CompilerParams(dimension_semantics=("parallel",),
                                             vmem_limit_bytes=VMEM_LIMIT),
        name="peer_v_dense",
    )(rows, coef_t, x1)


def _pack_table(t):
    e, d = t.shape
    tb = lax.bitcast_convert_type(t.astype(BF16), jnp.uint16).astype(jnp.uint32)
    packed = tb[:, :d // 2] | (tb[:, d // 2:] << 16)
    return packed.reshape(e, d // (2 * LANES), LANES)


def _tile(n, pref):
    t = min(n, pref)
    assert n % t == 0, (n, pref)
    return t


def kernel(x, mem, attn_norm_g, w_in, conv_w, conv_b, conv_ln_g, conv_ln_b, sb_out_g, mem_norm_g, w_mem_kv, q_norm_g, k_norm_g, mem_out_g, w_out, ffn_norm_g, w_peer_q, peer_keys1, peer_keys2, peer_u, peer_v):
    batch, seq, d = x.shape
    depth = w_in.shape[0]
    conv_ch = conv_w.shape[2]
    sb_width = sb_out_g.shape[1]
    mem_width = mem_out_g.shape[1]
    assert d == SUBLANES * LANES and PEER_TOPK == 2 * SUBLANES and conv_ch % LANES == 0 and sb_width % LANES == 0
    n = batch * seq
    q_off, k_off, v_off = 2 * conv_ch, 2 * conv_ch + sb_width, 2 * conv_ch + 2 * sb_width
    m_off = 2 * conv_ch + 3 * sb_width
    row = lambda v: v.reshape(1, -1)

    x2 = x.reshape(n, d)
    for l in range(depth):
        proj = _in_proj(x2, row(attn_norm_g[l]), w_in[l].astype(BF16), _tile(n, 512))
        conv_o = _conv(proj, conv_w[l], row(conv_b[l]), row(conv_ln_g[l]), row(conv_ln_b[l]),
                       batch, seq, _tile(seq, 512))
        sb = _stickbreak(proj, batch, seq, _tile(seq, 256), q_off // LANES, k_off // LANES,
                         v_off // LANES, sb_width // LANES)
        heads_m = mem_width // HEAD_DIM
        mk, mv = _mem_kv(mem, row(mem_norm_g[l]), w_mem_kv[l].astype(BF16),
                         row(jnp.tile(k_norm_g[l], heads_m)))
        mem_o = _mem_attn(proj, mk, mv, row(jnp.tile(q_norm_g[l], heads_m)), row(mem_out_g[l]),
                          batch, seq, _tile(seq, 512), m_off // mem_width)
        x1, h2, q = _out_proj(conv_o, sb, mem_o, x2, row(sb_out_g[l]), w_out[l].astype(BF16),
                              row(ffn_norm_g[l]), w_peer_q[l].astype(BF16), _tile(n, 512))
        eidx, gates = _peer_topk(q, peer_keys1[l].astype(BF16), peer_keys2[l].astype(BF16),
                                 _tile(n, 512))
        tg = _tile(n, 128)
        eidx_t = eidx.T
        v_rows = _sc_gather_rows(_pack_table(peer_v[l]).reshape(peer_v.shape[1], d // 2),
                                 eidx_t.reshape(-1))
        coef = _peer_u(eidx_t, _pack_table(peer_u[l]), h2.reshape(n, SUBLANES, LANES), gates, tg)
        x2 = _peer_v_dense(v_rows, coef.T, x1, _tile(n, 32))
    return x2.reshape(batch, seq, d)
```

```python
import functools

import jax
import jax.numpy as jnp
from jax import lax
from jax.experimental import pallas as pl
from jax.experimental.pallas import tpu as pltpu
from jax.experimental.pallas import tpu_sc as plsc

F32 = jnp.float32
BF16 = jnp.bfloat16
EPS = 1e-6

HEAD_DIM = 64
LANES = 128
SUBLANES = 8
BF16_ROWS = 16
ROW_WORDS = 4
CONV_HALO = 32
SB_ROWS = 128
SB_DEAD = -120.0
SC_CORES, SC_SUBCORES = 2, 16
SC_ROWS = 64
SC_IDX = 4096
PEER_U_GROUP = 8
PEER_TOPK = 16
N_KEYS = 128
NEG_INF = float("-inf")
BIG_ID = float(2 ** 24)
VMEM_LIMIT = 56 * 1024 * 1024


def _rms(x, g):
    return x * lax.rsqrt(jnp.mean(x * x, axis=-1, keepdims=True) + EPS) * g


def _dot_nt(a, b):
    return lax.dot_general(a, b, (((1,), (1,)), ((), ())), preferred_element_type=F32)


def _split_bf16(x):
    hi = x.astype(BF16)
    lo = (x - hi.astype(F32)).astype(BF16)
    return hi, lo


def _in_proj_kernel(x_ref, g_ref, w_ref, o_ref):
    h = _rms(x_ref[...], g_ref[...])
    o_ref[...] = jnp.dot(h.astype(BF16), w_ref[...], preferred_element_type=F32).astype(o_ref.dtype)


def _in_proj(x2, g, w, tm):
    n, d = x2.shape
    width = w.shape[1]
    return pl.pallas_call(
        _in_proj_kernel,
        grid=(n // tm,),
        in_specs=[pl.BlockSpec((tm, d), lambda i: (i, 0)),
                  pl.BlockSpec((1, d), lambda i: (0, 0)),
                  pl.BlockSpec((d, width), lambda i: (0, 0))],
        out_specs=pl.BlockSpec((tm, width), lambda i: (i, 0)),
        out_shape=jax.ShapeDtypeStruct((n, width), BF16),
        compiler_params=pltpu.CompilerParams(dimension_semantics=("parallel",),
                                             vmem_limit_bytes=VMEM_LIMIT),
        name="in_proj",
    )(x2, g, w)


def _conv_kernel(a_ref, gate_ref, w_ref, b_ref, lg_ref, lb_ref, o_ref, ubuf, *, tc, width):
    i = pl.program_id(1)

    @pl.when(i == 0)
    def _():
        ubuf[0:CONV_HALO, :] = jnp.zeros((CONV_HALO, ubuf.shape[1]), F32)

    @pl.when(i > 0)
    def _():
        ubuf[0:CONV_HALO, :] = ubuf[tc:tc + CONV_HALO, :]

    a = a_ref[...].astype(F32)
    gate = gate_ref[...].astype(F32)
    ubuf[CONV_HALO:CONV_HALO + tc, :] = a * jax.nn.sigmoid(gate)
    base = CONV_HALO - (width - 1)
    acc = jnp.zeros((tc, ubuf.shape[1]), F32)
    for j in range(width):
        acc = acc + w_ref[j:j + 1, :] * ubuf[base + j:base + j + tc, :]
    y = acc + b_ref[...]
    mu = jnp.mean(y, axis=-1, keepdims=True)
    yc = y - mu
    var = jnp.mean(yc * yc, axis=-1, keepdims=True)
    z = yc * lax.rsqrt(var + EPS) * lg_ref[...] + lb_ref[...]
    o_ref[...] = (z * jax.nn.sigmoid(z)).astype(o_ref.dtype)


def _conv(proj, conv_w, conv_b, ln_g, ln_b, batch, seq, tc):
    n = proj.shape[0]
    width, ch = conv_w.shape
    nt = seq // tc
    kern = functools.partial(_conv_kernel, tc=tc, width=width)
    vec = lambda: pl.BlockSpec((1, ch), lambda b, i: (0, 0))
    return pl.pallas_call(
        kern,
        grid=(batch, nt),
        in_specs=[pl.BlockSpec((tc, ch), lambda b, i: (b * nt + i, 0)),
                  pl.BlockSpec((tc, ch), lambda b, i: (b * nt + i, 1)),
                  pl.BlockSpec((width, ch), lambda b, i: (0, 0)),
                  vec(), vec(), vec()],
        out_specs=pl.BlockSpec((tc, ch), lambda b, i: (b * nt + i, 0)),
        out_shape=jax.ShapeDtypeStruct((n, ch), BF16),
        scratch_shapes=[pltpu.VMEM((tc + CONV_HALO, ch), F32)],
        compiler_params=pltpu.CompilerParams(dimension_semantics=("parallel", "arbitrary"),
                                             vmem_limit_bytes=VMEM_LIMIT),
        name="conv",
    )(proj, proj, conv_w, conv_b, ln_g, ln_b)


def _sb_kernel(q_ref, k_ref, v_ref, o_ref, *scratch, tq):
    n_chain = len(scratch) // 2
    acc_refs, carry_refs = scratch[:n_chain], scratch[n_chain:]
    qi = pl.program_id(2)
    q2 = q_ref[...]
    lane = lax.broadcasted_iota(jnp.int32, (1, LANES), 1)
    row = lax.broadcasted_iota(jnp.int32, (tq, tq), 0)
    col = lax.broadcasted_iota(jnp.int32, (tq, tq), 1)
    suffix = jnp.where(row > col, -1.0, 0.0).astype(BF16)
    scale = jnp.asarray(HEAD_DIM ** -0.5, q2.dtype)
    chains = []
    for h in range(2):
        head = (lane >= h * HEAD_DIM) & (lane < (h + 1) * HEAD_DIM)
        qh = jnp.where(head, q2, jnp.zeros_like(q2)) * scale
        for r0 in range(0, tq, SB_ROWS):
            chains.append((qh[r0:r0 + SB_ROWS], r0))
    for c in range(len(chains)):
        acc_refs[c][...] = jnp.zeros(acc_refs[c].shape, F32)
        carry_refs[c][...] = jnp.zeros(carry_refs[c].shape, F32)

    def stages(kb, diag):
        st = {}
        n = len(chains)

        def qk():
            start = pl.multiple_of(kb * tq, tq)
            st["k"] = k_ref[pl.ds(start, tq), :]
            st["v"] = v_ref[pl.ds(start, tq), :]
            st["z"] = [_dot_nt(qc, st["k"]) for qc, _ in chains]

        def logs():
            st["lg"], st["spb"], st["keep"] = [], [], []
            for z, (_, r0) in zip(st["z"], chains):
                lg = jnp.minimum(z, 0.0) - jnp.log(1.0 + jnp.exp(jnp.minimum(z, -z)))
                sp = z - lg
                if diag:
                    keep = (lax.broadcasted_iota(jnp.int32, (SB_ROWS, tq), 1)
                            < lax.broadcasted_iota(jnp.int32, (SB_ROWS, tq), 0) + r0)
                    sp = jnp.where(keep, sp, 0.0)
                    st["keep"].append(keep)
                st["lg"].append(lg)
                st["spb"].append(sp.astype(BF16))

        def sums():
            st["r"] = [jnp.dot(spb, suffix, preferred_element_type=F32) for spb in st["spb"]]

        def weights():
            st["a"] = []
            for c in range(n):
                after = st["r"][c] + carry_refs[c][...]
                a = jnp.exp(st["lg"][c] + after)
                if diag:
                    a = jnp.where(st["keep"][c], a, 0.0)
                st["a"].append(a.astype(BF16))
                carry_refs[c][...] = after[:, 0:1] - st["spb"][c][:, 0:LANES].astype(F32)[:, 0:1]

        def values():
            for c in range(n):
                acc_refs[c][...] += jnp.dot(st["a"][c], st["v"], preferred_element_type=F32)

        return [qk, logs, sums, weights, values]

    def run(blocks):
        seqs = [stages(kb, diag) for kb, diag in blocks]
        depth = len(seqs[0])
        for step in range(depth + len(seqs) - 1):
            for i, sq in enumerate(seqs):
                if 0 <= step - i < depth:
                    sq[step - i]()

    run([(qi, True)])

    def alive():
        top = functools.reduce(jnp.maximum, [jnp.max(r[...]) for r in carry_refs])
        return top > SB_DEAD

    @pl.when(qi >= 1)
    def _():
        run([(qi - 1, False)])

    rest = jnp.maximum(qi - 1, 0)

    def live(s):
        i, go = s
        return (i < rest // 2) & go

    def pair(s):
        i, _ = s
        run([(qi - 2 - 2 * i, False), (qi - 3 - 2 * i, False)])
        return i + 1, alive()

    _, go = lax.while_loop(live, pair, (jnp.int32(0), alive()))

    @pl.when((rest % 2 == 1) & go)
    def _():
        run([(0, False)])

    n_sub = tq // SB_ROWS
    out = [jnp.concatenate([acc_refs[h * n_sub + s][...] for s in range(n_sub)], axis=0) for h in range(2)]
    o_ref[...] = jnp.where(lane < HEAD_DIM, out[0], out[1]).astype(o_ref.dtype)


def _stickbreak(proj, batch, seq, tq, q_col, k_col, v_col, n_pairs):
    n = proj.shape[0]
    nq = seq // tq
    kern = functools.partial(_sb_kernel, tq=tq)
    return pl.pallas_call(
        kern,
        grid=(batch, n_pairs, nq),
        in_specs=[pl.BlockSpec((tq, LANES), lambda b, p, i: (b * nq + i, q_col + p)),
                  pl.BlockSpec((seq, LANES), lambda b, p, i: (b, k_col + p)),
                  pl.BlockSpec((seq, LANES), lambda b, p, i: (b, v_col + p))],
        out_specs=pl.BlockSpec((tq, LANES), lambda b, p, i: (b * nq + i, p)),
        out_shape=jax.ShapeDtypeStruct((n, n_pairs * LANES), F32),
        scratch_shapes=([pltpu.VMEM((SB_ROWS, LANES), F32)] * (2 * tq // SB_ROWS)
                        + [pltpu.VMEM((SB_ROWS, 1), F32)] * (2 * tq // SB_ROWS)),
        compiler_params=pltpu.CompilerParams(
            dimension_semantics=("parallel", "parallel", "arbitrary"),
            vmem_limit_bytes=VMEM_LIMIT),
        name="stickbreak",
    )(proj, proj, proj)


def _head_sumsq(x, blk_ones):
    hi, lo = _split_bf16(x * x)
    return (jnp.dot(hi, blk_ones, preferred_element_type=F32)
            + jnp.dot(lo, blk_ones, preferred_element_type=F32))


def _block_ones(width):
    r = lax.broadcasted_iota(jnp.int32, (width, width), 0) // HEAD_DIM
    c = lax.broadcasted_iota(jnp.int32, (width, width), 1) // HEAD_DIM
    return jnp.where(r == c, 1.0, 0.0).astype(BF16)


def _mem_kv_kernel(mem_ref, g_ref, w_ref, kg_ref, k_ref, v_ref, *, width):
    mn = _rms(mem_ref[...], g_ref[...])
    kv = jnp.dot(mn.astype(BF16), w_ref[...], preferred_element_type=F32)
    k = kv[:, :width]
    ss = _head_sumsq(k, _block_ones(width))
    k_ref[...] = (k * lax.rsqrt(ss * (1.0 / HEAD_DIM) + EPS) * kg_ref[...]).astype(k_ref.dtype)
    v_ref[...] = kv[:, width:].astype(v_ref.dtype)


def _mem_kv(mem, g, w, kg_t):
    b, m, d = mem.shape
    width = w.shape[1] // 2
    kern = functools.partial(_mem_kv_kernel, width=width)
    out = jax.ShapeDtypeStruct((b, m, width), BF16)
    return pl.pallas_call(
        kern,
        grid=(b,),
        in_specs=[pl.BlockSpec((None, m, d), lambda i: (i, 0, 0)),
                  pl.BlockSpec((1, d), lambda i: (0, 0)),
                  pl.BlockSpec((d, 2 * width), lambda i: (0, 0)),
                  pl.BlockSpec((1, width), lambda i: (0, 0))],
        out_specs=[pl.BlockSpec((None, m, width), lambda i: (i, 0, 0)),
                   pl.BlockSpec((None, m, width), lambda i: (i, 0, 0))],
        out_shape=[out, out],
        compiler_params=pltpu.CompilerParams(dimension_semantics=("parallel",),
                                             vmem_limit_bytes=VMEM_LIMIT),
        name="mem_kv",
    )(mem, g, w, kg_t)


def _mem_attn_kernel(q_ref, k_ref, v_ref, qg_ref, og_ref, o_ref, *, width):
    q = q_ref[...].astype(F32)
    ss = _head_sumsq(q, _block_ones(width))
    qn = q * lax.rsqrt(ss * (1.0 / HEAD_DIM) + EPS) * qg_ref[...]
    k = k_ref[...]
    v = v_ref[...]
    lane = lax.broadcasted_iota(jnp.int32, (1, width), 1)
    out = jnp.zeros(q.shape, F32)
    for h in range(width // HEAD_DIM):
        head = (lane >= h * HEAD_DIM) & (lane < (h + 1) * HEAD_DIM)
        qh = jnp.where(head, qn, 0.0).astype(BF16)
        s = _dot_nt(qh, k) * (HEAD_DIM ** -0.5)
        p = jnp.exp(s - jnp.max(s, axis=-1, keepdims=True))
        o = jnp.dot(p.astype(BF16), v, preferred_element_type=F32) / jnp.sum(p, axis=-1, keepdims=True)
        out = jnp.where(head, o, out)
    o_ref[...] = _rms(out, og_ref[...]).astype(o_ref.dtype)


def _mem_attn(proj, k, v, qg_t, og, batch, seq, tm, q_col):
    n = proj.shape[0]
    _, m, width = k.shape
    nt = seq // tm
    kern = functools.partial(_mem_attn_kernel, width=width)
    return pl.pallas_call(
        kern,
        grid=(batch, nt),
        in_specs=[pl.BlockSpec((tm, width), lambda b, i: (b * nt + i, q_col)),
                  pl.BlockSpec((None, m, width), lambda b, i: (b, 0, 0)),
                  pl.BlockSpec((None, m, width), lambda b, i: (b, 0, 0)),
                  pl.BlockSpec((1, width), lambda b, i: (0, 0)),
                  pl.BlockSpec((1, width), lambda b, i: (0, 0))],
        out_specs=pl.BlockSpec((tm, width), lambda b, i: (b * nt + i, 0)),
        out_shape=jax.ShapeDtypeStruct((n, width), BF16),
        compiler_params=pltpu.CompilerParams(dimension_semantics=("parallel", "parallel"),
                                             vmem_limit_bytes=VMEM_LIMIT),
        name="mem_attn",
    )(proj, k, v, qg_t, og)


def _out_proj_kernel(conv_ref, sb_ref, mem_ref, x_ref, sbg_ref, wo_ref, fg_ref, wq_ref,
                     x1_ref, h2_ref, q_ref, *, c_w, s_w):
    sbn = _rms(sb_ref[...], sbg_ref[...]).astype(BF16)
    mixed = jnp.dot(conv_ref[...], wo_ref[0:c_w, :], preferred_element_type=F32)
    mixed = mixed + jnp.dot(sbn, wo_ref[c_w:c_w + s_w, :], preferred_element_type=F32)
    mixed = mixed + jnp.dot(mem_ref[...], wo_ref[c_w + s_w:, :], preferred_element_type=F32)
    x1 = x_ref[...] + mixed
    h2 = _rms(x1, fg_ref[...])
    x1_ref[...] = x1
    h2_ref[...] = h2
    q_ref[...] = jnp.dot(h2.astype(BF16), wq_ref[...], preferred_element_type=F32).astype(q_ref.dtype)


def _out_proj(conv_o, sb, mem_o, x2, sbg, w_out, fg, w_q, tm):
    n, d = x2.shape
    c_w, s_w, m_w = conv_o.shape[1], sb.shape[1], mem_o.shape[1]
    qw = w_q.shape[1]
    kern = functools.partial(_out_proj_kernel, c_w=c_w, s_w=s_w)
    row = lambda w: pl.BlockSpec((tm, w), lambda i: (i, 0))
    full = lambda r, c: pl.BlockSpec((r, c), lambda i: (0, 0))
    return pl.pallas_call(
        kern,
        grid=(n // tm,),
        in_specs=[row(c_w), row(s_w), row(m_w), row(d), full(1, s_w), full(c_w + s_w + m_w, d),
                  full(1, d), full(d, qw)],
        out_specs=[row(d), row(d), row(qw)],
        out_shape=[jax.ShapeDtypeStruct((n, d), F32), jax.ShapeDtypeStruct((n, d), F32),
                   jax.ShapeDtypeStruct((n, qw), BF16)],
        compiler_params=pltpu.CompilerParams(dimension_semantics=("parallel",),
                                             vmem_limit_bytes=VMEM_LIMIT),
        name="out_proj",
    )(conv_o, sb, mem_o, x2, sbg, w_out, fg, w_q)


def _topk_rows(s, ids, k):
    vals, sel = [], []
    for _ in range(k):
        m = jnp.max(s, axis=0, keepdims=True)
        i = jnp.min(jnp.where(s == m, ids, BIG_ID), axis=0, keepdims=True)
        s = jnp.where(ids == i, NEG_INF, s)
        vals.append(m)
        sel.append(i)
    return jnp.concatenate(vals, axis=0), jnp.concatenate(sel, axis=0)


def _peer_topk_kernel(q_ref, k1_ref, k2_ref, eidx_ref, gates_ref, *, half):
    t = q_ref.shape[0]
    key_ids = lax.broadcasted_iota(jnp.int32, (N_KEYS, t), 0).astype(F32)
    q = q_ref[...]
    v1, i1 = _topk_rows(_dot_nt(k1_ref[...], q[:, :half]), key_ids, PEER_TOPK)
    v2, i2 = _topk_rows(_dot_nt(k2_ref[...], q[:, half:]), key_ids, PEER_TOPK)
    cand_s, cand_i = [v1[0:1] + v2], [i1[0:1] * N_KEYS + i2]
    sub = lax.broadcasted_iota(jnp.int32, (SUBLANES, t), 0)
    for a in range(1, SUBLANES):
        ok = sub < (PEER_TOPK // (a + 1))
        cand_s.append(jnp.where(ok, v1[a:a + 1] + v2[0:SUBLANES], NEG_INF))
        cand_i.append(i1[a:a + 1] * N_KEYS + i2[0:SUBLANES])
    cand_s.append(v1[SUBLANES:] + v2[0:1])
    cand_i.append(i1[SUBLANES:] * N_KEYS + i2[0:1])
    top_s, eidx = _topk_rows(jnp.concatenate(cand_s, axis=0), jnp.concatenate(cand_i, axis=0), PEER_TOPK)
    p = jnp.exp(top_s - top_s[0:1])
    eidx_ref[...] = eidx.astype(jnp.int32)
    gates_ref[...] = p / jnp.sum(p, axis=0, keepdims=True)


def _peer_topk(q, k1, k2, tt):
    n, qw = q.shape
    heads, n_keys, half = k1.shape
    kern = functools.partial(_peer_topk_kernel, half=half)
    out_spec = lambda: pl.BlockSpec((PEER_TOPK, tt), lambda i, h: (h, i))
    return pl.pallas_call(
        kern,
        grid=(n // tt, heads),
        in_specs=[pl.BlockSpec((tt, 2 * half), lambda i, h: (i, h)),
                  pl.BlockSpec((None, n_keys, half), lambda i, h: (h, 0, 0)),
                  pl.BlockSpec((None, n_keys, half), lambda i, h: (h, 0, 0))],
        out_specs=[out_spec(), out_spec()],
        out_shape=[jax.ShapeDtypeStruct((heads * PEER_TOPK, n), jnp.int32),
                   jax.ShapeDtypeStruct((heads * PEER_TOPK, n), F32)],
        compiler_params=pltpu.CompilerParams(dimension_semantics=("parallel", "parallel"),
                                             vmem_limit_bytes=VMEM_LIMIT),
        name="peer_topk",
    )(q, k1, k2)


def _unpack_pair(w):
    lo = pltpu.bitcast(w << 16, F32)
    hi = pltpu.bitcast(w & jnp.uint32(0xFFFF0000), F32)
    return lo, hi


def _spread_matrices(picks):
    pr = lax.broadcasted_iota(jnp.int32, (picks, 2 * picks), 0)
    pc = lax.broadcasted_iota(jnp.int32, (picks, 2 * picks), 1)
    return (jnp.where(pc == 2 * pr, 1.0, 0.0).astype(BF16),
            jnp.where(pc == 2 * pr + 1, 1.0, 0.0).astype(BF16))


def _dense_parts(coef):
    hi, lo = _split_bf16(coef)
    return [jnp.dot(x, s, preferred_element_type=F32)
            for s in _spread_matrices(coef.shape[1]) for x in (hi, lo)]


def _dense_token(rows, part_rows, x1_row):
    b = pltpu.bitcast(rows, BF16)
    sub = lax.broadcasted_iota(jnp.int32, (BF16_ROWS, b.shape[0]), 0)
    lhs = jnp.zeros((BF16_ROWS, b.shape[0]), F32)
    for k, row in enumerate(part_rows):
        lhs = jnp.where(sub == k, row, lhs)
    r = jnp.dot(lhs.astype(BF16), b, preferred_element_type=F32)
    return jnp.concatenate([r[0:1] + r[1:2], r[2:3] + r[3:4]], axis=1) + x1_row


def _peer_u_kernel(idx_ref, tbl_ref, h2_ref, gates_ref, coef_ref, act_ref, *y_refs, tg, picks):
    lane = lax.broadcasted_iota(jnp.int32, (1, tg), 1)
    group = len(y_refs)

    def products(t, y_ref):
        hv = h2_ref[t]
        h_lo, h_hi = hv[0:ROW_WORDS], hv[ROW_WORDS:]
        for p in range(picks):
            lo, hi = _unpack_pair(tbl_ref[idx_ref[t, p]])
            y_ref[ROW_WORDS * p:ROW_WORDS * (p + 1), :] = lo * h_lo + hi * h_hi

    def reduce(t, y_ref):
        part = functools.reduce(jnp.add, [y_ref[pl.ds(r, picks, stride=ROW_WORDS), :]
                                          for r in range(ROW_WORDS)])
        act = jnp.sum(part, axis=1, keepdims=True)
        act_ref[...] = jnp.where(lane == t, act, act_ref[...])

    y_refs[-1][...] = jnp.zeros(y_refs[-1].shape, F32)
    act_ref[...] = jnp.zeros(act_ref.shape, F32)

    def body(j, c):
        t0 = group * j
        for k in range(group):
            products(t0 + k, y_refs[k])
            reduce(t0 + k - 1, y_refs[k - 1])
        return c

    lax.fori_loop(0, tg // group, body, 0)
    reduce(tg - 1, y_refs[-1])
    act = act_ref[...]
    gelu = 0.5 * act * (1.0 + lax.erf(act * (2.0 ** -0.5)))
    coef_ref[...] = gates_ref[...] * gelu


def _peer_u(eidx, tbl, h2r, gates, tg):
    n, picks = eidx.shape
    kern = functools.partial(_peer_u_kernel, tg=tg, picks=picks)
    return pl.pallas_call(
        kern,
        grid=(n // tg,),
        in_specs=[pl.BlockSpec((tg, picks), lambda i: (i, 0), memory_space=pltpu.SMEM,
                               pipeline_mode=pl.Buffered(1)),
                  pl.BlockSpec(memory_space=pltpu.VMEM),
                  pl.BlockSpec((tg, SUBLANES, LANES), lambda i: (i, 0, 0)),
                  pl.BlockSpec((picks, tg), lambda i: (0, i))],
        out_specs=pl.BlockSpec((picks, tg), lambda i: (0, i)),
        out_shape=jax.ShapeDtypeStruct((picks, n), F32),
        scratch_shapes=([pltpu.VMEM((picks, tg), F32)]
                        + [pltpu.VMEM((ROW_WORDS * picks, LANES), F32)] * PEER_U_GROUP),
        compiler_params=pltpu.CompilerParams(dimension_semantics=("parallel",),
                                             vmem_limit_bytes=VMEM_LIMIT),
        name="peer_u",
    )(eidx, tbl, h2r, gates)


def _sc_gather_rows(table, idx):
    b, w = idx.shape[0], table.shape[1]
    workers = SC_CORES * SC_SUBCORES
    per_w = b // workers
    step = 2 * SC_ROWS
    assert b % workers == 0 and per_w % SC_IDX == 0 and SC_IDX % step == 0
    mesh = plsc.VectorSubcoreMesh(core_axis_name="c", subcore_axis_name="s")

    def body(table_hbm, idx_hbm, out_hbm, idx_v, rows0, rows1, g0, g1, w0, w1):
        wid = lax.axis_index("s") * SC_CORES + lax.axis_index("c")
        base = wid * per_w

        @pl.loop(0, per_w // SC_IDX)
        def _(o):
            off = pl.multiple_of(base + o * SC_IDX, SC_IDX)
            pltpu.sync_copy(idx_hbm.at[pl.ds(off, SC_IDX)], idx_v)

            @pl.loop(0, SC_IDX // step)
            def _(j):
                r0 = pl.multiple_of(j * step, step)
                r1 = pl.multiple_of(r0 + SC_ROWS, SC_ROWS)
                ga = pltpu.async_copy(table_hbm.at[idx_v.at[pl.ds(r0, SC_ROWS)]], rows0, g0)
                gb = pltpu.async_copy(table_hbm.at[idx_v.at[pl.ds(r1, SC_ROWS)]], rows1, g1)
                ga.wait()
                wa = pltpu.async_copy(rows0, out_hbm.at[pl.ds(pl.multiple_of(off + r0, SC_ROWS), SC_ROWS)], w0)
                gb.wait()
                wb = pltpu.async_copy(rows1, out_hbm.at[pl.ds(pl.multiple_of(off + r1, SC_ROWS), SC_ROWS)], w1)
                wa.wait()
                wb.wait()

    return pl.kernel(
        body,
        out_type=jax.ShapeDtypeStruct((b, w), table.dtype),
        mesh=mesh,
        scratch_types=[pltpu.VMEM((SC_IDX,), jnp.int32), pltpu.VMEM((SC_ROWS, w), table.dtype),
                       pltpu.VMEM((SC_ROWS, w), table.dtype)] + [pltpu.SemaphoreType.DMA] * 4,
        name="sc_gather_rows",
    )(table, idx)


def _peer_v_dense_kernel(rows_ref, coef_ref, x1_ref, o_ref, *, tt, picks):
    parts = _dense_parts(coef_ref[...])
    for t in range(tt):
        o_ref[t:t + 1, :] = _dense_token(rows_ref[t * picks:(t + 1) * picks, :],
                                         [part[t:t + 1, :] for part in parts], x1_ref[t:t + 1, :])


def _peer_v_dense(rows, coef_t, x1, tt):
    n, picks = coef_t.shape
    d = x1.shape[1]
    kern = functools.partial(_peer_v_dense_kernel, tt=tt, picks=picks)
    return pl.pallas_call(
        kern,
        grid=(n // tt,),
        in_specs=[pl.BlockSpec((tt * picks, d // 2), lambda i: (i, 0)),
                  pl.BlockSpec((tt, picks), lambda i: (i, 0)),
                  pl.BlockSpec((tt, d), lambda i: (i, 0))],
        out_specs=pl.BlockSpec((tt, d), lambda i: (i, 0)),
        out_shape=jax.ShapeDtypeStruct((n, d), F32),
        compiler_params=pltpu.CompilerParams(dimension_semantics=("parallel",),
                                             vmem_limit_bytes=VMEM_LIMIT),
        name="peer_v_dense",
    )(rows, coef_t, x1)


def _pack_table(t):
    e, d = t.shape
    tb = lax.bitcast_convert_type(t.astype(BF16), jnp.uint16).astype(jnp.uint32)
    packed = tb[:, :d // 2] | (tb[:, d // 2:] << 16)
    return packed.reshape(e, d // (2 * LANES), LANES)


def _tile(n, pref):
    t = min(n, pref)
    assert n % t == 0, (n, pref)
    return t


def kernel(x, mem, attn_norm_g, w_in, conv_w, conv_b, conv_ln_g, conv_ln_b, sb_out_g, mem_norm_g, w_mem_kv, q_norm_g, k_norm_g, mem_out_g, w_out, ffn_norm_g, w_peer_q, peer_keys1, peer_keys2, peer_u, peer_v):
    batch, seq, d = x.shape
    depth = w_in.shape[0]
    conv_ch = conv_w.shape[2]
    sb_width = sb_out_g.shape[1]
    mem_width = mem_out_g.shape[1]
    assert d == SUBLANES * LANES and PEER_TOPK == 2 * SUBLANES and conv_ch % LANES == 0 and sb_width % LANES == 0
    n = batch * seq
    q_off, k_off, v_off = 2 * conv_ch, 2 * conv_ch + sb_width, 2 * conv_ch + 2 * sb_width
    m_off = 2 * conv_ch + 3 * sb_width
    row = lambda v: v.reshape(1, -1)

    x2 = x.reshape(n, d)
    for l in range(depth):
        proj = _in_proj(x2, row(attn_norm_g[l]), w_in[l].astype(BF16), _tile(n, 512))
        conv_o = _conv(proj, conv_w[l], row(conv_b[l]), row(conv_ln_g[l]), row(conv_ln_b[l]),
                       batch, seq, _tile(seq, 512))
        sb = _stickbreak(proj, batch, seq, _tile(seq, 256), q_off // LANES, k_off // LANES,
                         v_off // LANES, sb_width // LANES)
        heads_m = mem_width // HEAD_DIM
        mk, mv = _mem_kv(mem, row(mem_norm_g[l]), w_mem_kv[l].astype(BF16),
                         row(jnp.tile(k_norm_g[l], heads_m)))
        mem_o = _mem_attn(proj, mk, mv, row(jnp.tile(q_norm_g[l], heads_m)), row(mem_out_g[l]),
                          batch, seq, _tile(seq, 512), m_off // mem_width)
        x1, h2, q = _out_proj(conv_o, sb, mem_o, x2, row(sb_out_g[l]), w_out[l].astype(BF16),
                              row(ffn_norm_g[l]), w_peer_q[l].astype(BF16), _tile(n, 512))
        eidx, gates = _peer_topk(q, peer_keys1[l].astype(BF16), peer_keys2[l].astype(BF16),
                                 _tile(n, 1024))
        tg = _tile(n, 256)
        eidx_t = eidx.T
        v_rows = _sc_gather_rows(_pack_table(peer_v[l]).reshape(peer_v.shape[1], d // 2),
                                 eidx_t.reshape(-1))
        coef = _peer_u(eidx_t, _pack_table(peer_u[l]), h2.reshape(n, SUBLANES, LANES), gates, tg)
        x2 = _peer_v_dense(v_rows, coef.T, x1, _tile(n, 32))
    return x2.reshape(batch, seq, d)
```

```python
import functools

import jax
import jax.numpy as jnp
from jax import lax
from jax.experimental import pallas as pl
from jax.experimental.pallas import tpu as pltpu
from jax.experimental.pallas import tpu_sc as plsc

F32 = jnp.float32
BF16 = jnp.bfloat16
EPS = 1e-6

HEAD_DIM = 64
LANES = 128
SUBLANES = 8
BF16_ROWS = 16
ROW_WORDS = 4
CONV_HALO = 32
SB_ROWS = 128
SB_DEAD = -120.0
SC_CORES, SC_SUBCORES = 2, 16
SC_ROWS = 64
SC_IDX = 4096
PEER_U_GROUP = 8
PEER_TOPK = 16
N_KEYS = 128
NEG_INF = float("-inf")
BIG_ID = float(2 ** 24)
VMEM_LIMIT = 56 * 1024 * 1024


def _rms(x, g):
    return x * lax.rsqrt(jnp.mean(x * x, axis=-1, keepdims=True) + EPS) * g


def _dot_nt(a, b):
    return lax.dot_general(a, b, (((1,), (1,)), ((), ())), preferred_element_type=F32)


def _split_bf16(x):
    hi = x.astype(BF16)
    lo = (x - hi.astype(F32)).astype(BF16)
    return hi, lo


def _in_proj_kernel(x_ref, g_ref, w_ref, o_ref):
    h = _rms(x_ref[...], g_ref[...])
    o_ref[...] = jnp.dot(h.astype(BF16), w_ref[...], preferred_element_type=F32).astype(o_ref.dtype)


def _in_proj(x2, g, w, tm):
    n, d = x2.shape
    width = w.shape[1]
    return pl.pallas_call(
        _in_proj_kernel,
        grid=(n // tm,),
        in_specs=[pl.BlockSpec((tm, d), lambda i: (i, 0)),
                  pl.BlockSpec((1, d), lambda i: (0, 0)),
                  pl.BlockSpec((d, width), lambda i: (0, 0))],
        out_specs=pl.BlockSpec((tm, width), lambda i: (i, 0)),
        out_shape=jax.ShapeDtypeStruct((n, width), BF16),
        compiler_params=pltpu.CompilerParams(dimension_semantics=("parallel",),
                                             vmem_limit_bytes=VMEM_LIMIT),
        name="in_proj",
    )(x2, g, w)


def _conv_kernel(a_ref, gate_ref, w_ref, b_ref, lg_ref, lb_ref, o_ref, ubuf, *, tc, width):
    i = pl.program_id(1)

    @pl.when(i == 0)
    def _():
        ubuf[0:CONV_HALO, :] = jnp.zeros((CONV_HALO, ubuf.shape[1]), F32)

    @pl.when(i > 0)
    def _():
        ubuf[0:CONV_HALO, :] = ubuf[tc:tc + CONV_HALO, :]

    a = a_ref[...].astype(F32)
    gate = gate_ref[...].astype(F32)
    ubuf[CONV_HALO:CONV_HALO + tc, :] = a * jax.nn.sigmoid(gate)
    base = CONV_HALO - (width - 1)
    acc = jnp.zeros((tc, ubuf.shape[1]), F32)
    for j in range(width):
        acc = acc + w_ref[j:j + 1, :] * ubuf[base + j:base + j + tc, :]
    y = acc + b_ref[...]
    mu = jnp.mean(y, axis=-1, keepdims=True)
    yc = y - mu
    var = jnp.mean(yc * yc, axis=-1, keepdims=True)
    z = yc * lax.rsqrt(var + EPS) * lg_ref[...] + lb_ref[...]
    o_ref[...] = (z * jax.nn.sigmoid(z)).astype(o_ref.dtype)


def _conv(proj, conv_w, conv_b, ln_g, ln_b, batch, seq, tc):
    n = proj.shape[0]
    width, ch = conv_w.shape
    nt = seq // tc
    kern = functools.partial(_conv_kernel, tc=tc, width=width)
    vec = lambda: pl.BlockSpec((1, ch), lambda b, i: (0, 0))
    return pl.pallas_call(
        kern,
        grid=(batch, nt),
        in_specs=[pl.BlockSpec((tc, ch), lambda b, i: (b * nt + i, 0)),
                  pl.BlockSpec((tc, ch), lambda b, i: (b * nt + i, 1)),
                  pl.BlockSpec((width, ch), lambda b, i: (0, 0)),
                  vec(), vec(), vec()],
        out_specs=pl.BlockSpec((tc, ch), lambda b, i: (b * nt + i, 0)),
        out_shape=jax.ShapeDtypeStruct((n, ch), BF16),
        scratch_shapes=[pltpu.VMEM((tc + CONV_HALO, ch), F32)],
        compiler_params=pltpu.CompilerParams(dimension_semantics=("parallel", "arbitrary"),
                                             vmem_limit_bytes=VMEM_LIMIT),
        name="conv",
    )(proj, proj, conv_w, conv_b, ln_g, ln_b)


def _sb_kernel(q_ref, k_ref, v_ref, o_ref, *scratch, tq):
    n_chain = len(scratch) // 2
    acc_refs, carry_refs = scratch[:n_chain], scratch[n_chain:]
    qi = pl.program_id(2)
    q2 = q_ref[...]
    lane = lax.broadcasted_iota(jnp.int32, (1, LANES), 1)
    row = lax.broadcasted_iota(jnp.int32, (tq, tq), 0)
    col = lax.broadcasted_iota(jnp.int32, (tq, tq), 1)
    suffix = jnp.where(row > col, -1.0, 0.0).astype(BF16)
    scale = jnp.asarray(HEAD_DIM ** -0.5, q2.dtype)
    chains = []
    for h in range(2):
        head = (lane >= h * HEAD_DIM) & (lane < (h + 1) * HEAD_DIM)
        qh = jnp.where(head, q2, jnp.zeros_like(q2)) * scale
        for r0 in range(0, tq, SB_ROWS):
            chains.append((qh[r0:r0 + SB_ROWS], r0))
    for c in range(len(chains)):
        acc_refs[c][...] = jnp.zeros(acc_refs[c].shape, F32)
        carry_refs[c][...] = jnp.zeros(carry_refs[c].shape, F32)

    def stages(kb, diag):
        st = {}
        n = len(chains)

        def qk():
            start = pl.multiple_of(kb * tq, tq)
            st["k"] = k_ref[pl.ds(start, tq), :]
            st["v"] = v_ref[pl.ds(start, tq), :]
            st["z"] = [_dot_nt(qc, st["k"]) for qc, _ in chains]

        def logs():
            st["lg"], st["spb"], st["keep"] = [], [], []
            for z, (_, r0) in zip(st["z"], chains):
                lg = jnp.minimum(z, 0.0) - jnp.log(1.0 + jnp.exp(jnp.minimum(z, -z)))
                sp = z - lg
                if diag:
                    keep = (lax.broadcasted_iota(jnp.int32, (SB_ROWS, tq), 1)
                            < lax.broadcasted_iota(jnp.int32, (SB_ROWS, tq), 0) + r0)
                    sp = jnp.where(keep, sp, 0.0)
                    st["keep"].append(keep)
                st["lg"].append(lg)
                st["spb"].append(sp.astype(BF16))

        def sums():
            st["r"] = [jnp.dot(spb, suffix, preferred_element_type=F32) for spb in st["spb"]]

        def weights():
            st["a"] = []
            for c in range(n):
                after = st["r"][c] + carry_refs[c][...]
                a = jnp.exp(st["lg"][c] + after)
                if diag:
                    a = jnp.where(st["keep"][c], a, 0.0)
                st["a"].append(a.astype(BF16))
                carry_refs[c][...] = after[:, 0:1] - st["spb"][c][:, 0:LANES].astype(F32)[:, 0:1]

        def values():
            for c in range(n):
                acc_refs[c][...] += jnp.dot(st["a"][c], st["v"], preferred_element_type=F32)

        return [qk, logs, sums, weights, values]

    def run(blocks):
        seqs = [stages(kb, diag) for kb, diag in blocks]
        depth = len(seqs[0])
        for step in range(depth + len(seqs) - 1):
            for i, sq in enumerate(seqs):
                if 0 <= step - i < depth:
                    sq[step - i]()

    run([(qi, True)])

    def alive():
        top = functools.reduce(jnp.maximum, [jnp.max(r[...]) for r in carry_refs])
        return top > SB_DEAD

    @pl.when(qi >= 1)
    def _():
        run([(qi - 1, False)])

    rest = jnp.maximum(qi - 1, 0)

    def live(s):
        i, go = s
        return (i < rest // 2) & go

    def pair(s):
        i, _ = s
        run([(qi - 2 - 2 * i, False), (qi - 3 - 2 * i, False)])
        return i + 1, alive()

    _, go = lax.while_loop(live, pair, (jnp.int32(0), alive()))

    @pl.when((rest % 2 == 1) & go)
    def _():
        run([(0, False)])

    n_sub = tq // SB_ROWS
    out = [jnp.concatenate([acc_refs[h * n_sub + s][...] for s in range(n_sub)], axis=0) for h in range(2)]
    o_ref[...] = jnp.where(lane < HEAD_DIM, out[0], out[1]).astype(o_ref.dtype)


def _stickbreak(proj, batch, seq, tq, q_col, k_col, v_col, n_pairs):
    n = proj.shape[0]
    nq = seq // tq
    kern = functools.partial(_sb_kernel, tq=tq)
    return pl.pallas_call(
        kern,
        grid=(batch, n_pairs, nq),
        in_specs=[pl.BlockSpec((tq, LANES), lambda b, p, i: (b * nq + i, q_col + p)),
                  pl.BlockSpec((seq, LANES), lambda b, p, i: (b, k_col + p)),
                  pl.BlockSpec((seq, LANES), lambda b, p, i: (b, v_col + p))],
        out_specs=pl.BlockSpec((tq, LANES), lambda b, p, i: (b * nq + i, p)),
        out_shape=jax.ShapeDtypeStruct((n, n_pairs * LANES), F32),
        scratch_shapes=([pltpu.VMEM((SB_ROWS, LANES), F32)] * (2 * tq // SB_ROWS)
                        + [pltpu.VMEM((SB_ROWS, 1), F32)] * (2 * tq // SB_ROWS)),
        compiler_params=pltpu.CompilerParams(
            dimension_semantics=("parallel", "parallel", "arbitrary"),
            vmem_limit_bytes=VMEM_LIMIT),
        name="stickbreak",
    )(proj, proj, proj)


def _head_sumsq(x, blk_ones):
    hi, lo = _split_bf16(x * x)
    return (jnp.dot(hi, blk_ones, preferred_element_type=F32)
            + jnp.dot(lo, blk_ones, preferred_element_type=F32))


def _block_ones(width):
    r = lax.broadcasted_iota(jnp.int32, (width, width), 0) // HEAD_DIM
    c = lax.broadcasted_iota(jnp.int32, (width, width), 1) // HEAD_DIM
    return jnp.where(r == c, 1.0, 0.0).astype(BF16)


def _mem_kv_kernel(mem_ref, g_ref, w_ref, kg_ref, k_ref, v_ref, *, width):
    mn = _rms(mem_ref[...], g_ref[...])
    kv = jnp.dot(mn.astype(BF16), w_ref[...], preferred_element_type=F32)
    k = kv[:, :width]
    ss = _head_sumsq(k, _block_ones(width))
    k_ref[...] = (k * lax.rsqrt(ss * (1.0 / HEAD_DIM) + EPS) * kg_ref[...]).astype(k_ref.dtype)
    v_ref[...] = kv[:, width:].astype(v_ref.dtype)


def _mem_kv(mem, g, w, kg_t):
    b, m, d = mem.shape
    width = w.shape[1] // 2
    kern = functools.partial(_mem_kv_kernel, width=width)
    out = jax.ShapeDtypeStruct((b, m, width), BF16)
    return pl.pallas_call(
        kern,
        grid=(b,),
        in_specs=[pl.BlockSpec((None, m, d), lambda i: (i, 0, 0)),
                  pl.BlockSpec((1, d), lambda i: (0, 0)),
                  pl.BlockSpec((d, 2 * width), lambda i: (0, 0)),
                  pl.BlockSpec((1, width), lambda i: (0, 0))],
        out_specs=[pl.BlockSpec((None, m, width), lambda i: (i, 0, 0)),
                   pl.BlockSpec((None, m, width), lambda i: (i, 0, 0))],
        out_shape=[out, out],
        compiler_params=pltpu.CompilerParams(dimension_semantics=("parallel",),
                                             vmem_limit_bytes=VMEM_LIMIT),
        name="mem_kv",
    )(mem, g, w, kg_t)


def _mem_attn_kernel(q_ref, k_ref, v_ref, qg_ref, og_ref, o_ref, *, width):
    q = q_ref[...].astype(F32)
    ss = _head_sumsq(q, _block_ones(width))
    qn = q * lax.rsqrt(ss * (1.0 / HEAD_DIM) + EPS) * qg_ref[...]
    k = k_ref[...]
    v = v_ref[...]
    lane = lax.broadcasted_iota(jnp.int32, (1, width), 1)
    out = jnp.zeros(q.shape, F32)
    for h in range(width // HEAD_DIM):
        head = (lane >= h * HEAD_DIM) & (lane < (h + 1) * HEAD_DIM)
        qh = jnp.where(head, qn, 0.0).astype(BF16)
        s = _dot_nt(qh, k) * (HEAD_DIM ** -0.5)
        p = jnp.exp(s - jnp.max(s, axis=-1, keepdims=True))
        o = jnp.dot(p.astype(BF16), v, preferred_element_type=F32) / jnp.sum(p, axis=-1, keepdims=True)
        out = jnp.where(head, o, out)
    o_ref[...] = _rms(out, og_ref[...]).astype(o_ref.dtype)


def _mem_attn(proj, k, v, qg_t, og, batch, seq, tm, q_col):
    n = proj.shape[0]
    _, m, width = k.shape
    nt = seq // tm
    kern = functools.partial(_mem_attn_kernel, width=width)
    return pl.pallas_call(
        kern,
        grid=(batch, nt),
        in_specs=[pl.BlockSpec((tm, width), lambda b, i: (b * nt + i, q_col)),
                  pl.BlockSpec((None, m, width), lambda b, i: (b, 0, 0)),
                  pl.BlockSpec((None, m, width), lambda b, i: (b, 0, 0)),
                  pl.BlockSpec((1, width), lambda b, i: (0, 0)),
                  pl.BlockSpec((1, width), lambda b, i: (0, 0))],
        out_specs=pl.BlockSpec((tm, width), lambda b, i: (b * nt + i, 0)),
        out_shape=jax.ShapeDtypeStruct((n, width), BF16),
        compiler_params=pltpu.CompilerParams(dimension_semantics=("parallel", "parallel"),
                                             vmem_limit_bytes=VMEM_LIMIT),
        name="mem_attn",
    )(proj, k, v, qg_t, og)


def _out_proj_kernel(conv_ref, sb_ref, mem_ref, x_ref, sbg_ref, wo_ref, fg_ref, wq_ref,
                     x1_ref, h2_ref, q_ref, *, c_w, s_w):
    sbn = _rms(sb_ref[...], sbg_ref[...]).astype(BF16)
    mixed = jnp.dot(conv_ref[...], wo_ref[0:c_w, :], preferred_element_type=F32)
    mixed = mixed + jnp.dot(sbn, wo_ref[c_w:c_w + s_w, :], preferred_element_type=F32)
    mixed = mixed + jnp.dot(mem_ref[...], wo_ref[c_w + s_w:, :], preferred_element_type=F32)
    x1 = x_ref[...] + mixed
    h2 = _rms(x1, fg_ref[...])
    x1_ref[...] = x1
    h2_ref[...] = h2
    q_ref[...] = jnp.dot(h2.astype(BF16), wq_ref[...], preferred_element_type=F32).astype(q_ref.dtype)


def _out_proj(conv_o, sb, mem_o, x2, sbg, w_out, fg, w_q, tm):
    n, d = x2.shape
    c_w, s_w, m_w = conv_o.shape[1], sb.shape[1], mem_o.shape[1]
    qw = w_q.shape[1]
    kern = functools.partial(_out_proj_kernel, c_w=c_w, s_w=s_w)
    row = lambda w: pl.BlockSpec((tm, w), lambda i: (i, 0))
    full = lambda r, c: pl.BlockSpec((r, c), lambda i: (0, 0))
    return pl.pallas_call(
        kern,
        grid=(n // tm,),
        in_specs=[row(c_w), row(s_w), row(m_w), row(d), full(1, s_w), full(c_w + s_w + m_w, d),
                  full(1, d), full(d, qw)],
        out_specs=[row(d), row(d), row(qw)],
        out_shape=[jax.ShapeDtypeStruct((n, d), F32), jax.ShapeDtypeStruct((n, d), F32),
                   jax.ShapeDtypeStruct((n, qw), BF16)],
        compiler_params=pltpu.CompilerParams(dimension_semantics=("parallel",),
                                             vmem_limit_bytes=VMEM_LIMIT),
        name="out_proj",
    )(conv_o, sb, mem_o, x2, sbg, w_out, fg, w_q)


def _topk_rows(s, ids, k):
    vals, sel = [], []
    for _ in range(k):
        m = jnp.max(s, axis=0, keepdims=True)
        i = jnp.min(jnp.where(s == m, ids, BIG_ID), axis=0, keepdims=True)
        s = jnp.where(ids == i, NEG_INF, s)
        vals.append(m)
        sel.append(i)
    return jnp.concatenate(vals, axis=0), jnp.concatenate(sel, axis=0)


def _peer_topk_kernel(q_ref, k1_ref, k2_ref, eidx_ref, gates_ref, *, half):
    t = q_ref.shape[0]
    key_ids = lax.broadcasted_iota(jnp.int32, (N_KEYS, t), 0).astype(F32)
    q = q_ref[...]
    v1, i1 = _topk_rows(_dot_nt(k1_ref[...], q[:, :half]), key_ids, PEER_TOPK)
    v2, i2 = _topk_rows(_dot_nt(k2_ref[...], q[:, half:]), key_ids, PEER_TOPK)
    cand_s, cand_i = [v1[0:1] + v2], [i1[0:1] * N_KEYS + i2]
    sub = lax.broadcasted_iota(jnp.int32, (SUBLANES, t), 0)
    for a in range(1, SUBLANES):
        ok = sub < (PEER_TOPK // (a + 1))
        cand_s.append(jnp.where(ok, v1[a:a + 1] + v2[0:SUBLANES], NEG_INF))
        cand_i.append(i1[a:a + 1] * N_KEYS + i2[0:SUBLANES])
    cand_s.append(v1[SUBLANES:] + v2[0:1])
    cand_i.append(i1[SUBLANES:] * N_KEYS + i2[0:1])
    top_s, eidx = _topk_rows(jnp.concatenate(cand_s, axis=0), jnp.concatenate(cand_i, axis=0), PEER_TOPK)
    p = jnp.exp(top_s - top_s[0:1])
    eidx_ref[...] = eidx.astype(jnp.int32)
    gates_ref[...] = p / jnp.sum(p, axis=0, keepdims=True)


def _peer_topk(q, k1, k2, tt):
    n, qw = q.shape
    heads, n_keys, half = k1.shape
    kern = functools.partial(_peer_topk_kernel, half=half)
    out_spec = lambda: pl.BlockSpec((PEER_TOPK, tt), lambda i, h: (h, i))
    return pl.pallas_call(
        kern,
        grid=(n // tt, heads),
        in_specs=[pl.BlockSpec((tt, 2 * half), lambda i, h: (i, h)),
                  pl.BlockSpec((None, n_keys, half), lambda i, h: (h, 0, 0)),
                  pl.BlockSpec((None, n_keys, half), lambda i, h: (h, 0, 0))],
        out_specs=[out_spec(), out_spec()],
        out_shape=[jax.ShapeDtypeStruct((heads * PEER_TOPK, n), jnp.int32),
                   jax.ShapeDtypeStruct((heads * PEER_TOPK, n), F32)],
        compiler_params=pltpu.CompilerParams(dimension_semantics=("parallel", "parallel"),
                                             vmem_limit_bytes=VMEM_LIMIT),
        name="peer_topk",
    )(q, k1, k2)


def _unpack_pair(w):
    lo = pltpu.bitcast(w << 16, F32)
    hi = pltpu.bitcast(w & jnp.uint32(0xFFFF0000), F32)
    return lo, hi


def _spread_matrices(picks):
    pr = lax.broadcasted_iota(jnp.int32, (picks, 2 * picks), 0)
    pc = lax.broadcasted_iota(jnp.int32, (picks, 2 * picks), 1)
    return (jnp.where(pc == 2 * pr, 1.0, 0.0).astype(BF16),
            jnp.where(pc == 2 * pr + 1, 1.0, 0.0).astype(BF16))


def _dense_parts(coef):
    hi, lo = _split_bf16(coef)
    return [jnp.dot(x, s, preferred_element_type=F32)
            for s in _spread_matrices(coef.shape[1]) for x in (hi, lo)]


def _dense_token(rows, part_rows, x1_row):
    b = pltpu.bitcast(rows, BF16)
    sub = lax.broadcasted_iota(jnp.int32, (BF16_ROWS, b.shape[0]), 0)
    lhs = jnp.zeros((BF16_ROWS, b.shape[0]), F32)
    for k, row in enumerate(part_rows):
        lhs = jnp.where(sub == k, row, lhs)
    r = jnp.dot(lhs.astype(BF16), b, preferred_element_type=F32)
    return jnp.concatenate([r[0:1] + r[1:2], r[2:3] + r[3:4]], axis=1) + x1_row


def _peer_u_kernel(idx_ref, tbl_ref, h2_ref, gates_ref, coef_ref, act_ref, *y_refs, tg, picks):
    lane = lax.broadcasted_iota(jnp.int32, (1, LANES), 1)
    group = len(y_refs)

    def products(t, y_ref):
        hv = h2_ref[t]
        h_lo, h_hi = hv[0:ROW_WORDS], hv[ROW_WORDS:]
        for p in range(picks):
            lo, hi = _unpack_pair(tbl_ref[idx_ref[t, p]])
            y_ref[ROW_WORDS * p:ROW_WORDS * (p + 1), :] = lo * h_lo + hi * h_hi

    def reduce(t, y_ref):
        part = functools.reduce(jnp.add, [y_ref[pl.ds(r, picks, stride=ROW_WORDS), :]
                                          for r in range(ROW_WORDS)])
        act = jnp.sum(part, axis=1, keepdims=True)
        blk = pl.multiple_of(jnp.maximum(t, 0) // LANES * LANES, LANES)
        cols = pl.ds(blk, LANES)
        act_ref[:, cols] = jnp.where(lane == t - blk, act, act_ref[:, cols])

    y_refs[-1][...] = jnp.zeros(y_refs[-1].shape, F32)
    act_ref[...] = jnp.zeros(act_ref.shape, F32)

    def body(j, c):
        t0 = group * j
        for k in range(group):
            products(t0 + k, y_refs[k])
            reduce(t0 + k - 1, y_refs[k - 1])
        return c

    lax.fori_loop(0, tg // group, body, 0)
    reduce(tg - 1, y_refs[-1])
    act = act_ref[...]
    gelu = 0.5 * act * (1.0 + lax.erf(act * (2.0 ** -0.5)))
    coef_ref[...] = gates_ref[...] * gelu


def _peer_u(eidx, tbl, h2r, gates, tg):
    n, picks = eidx.shape
    kern = functools.partial(_peer_u_kernel, tg=tg, picks=picks)
    return pl.pallas_call(
        kern,
        grid=(n // tg,),
        in_specs=[pl.BlockSpec((tg, picks), lambda i: (i, 0), memory_space=pltpu.SMEM,
                               pipeline_mode=pl.Buffered(1)),
                  pl.BlockSpec(memory_space=pltpu.VMEM),
                  pl.BlockSpec((tg, SUBLANES, LANES), lambda i: (i, 0, 0)),
                  pl.BlockSpec((picks, tg), lambda i: (0, i))],
        out_specs=pl.BlockSpec((picks, tg), lambda i: (0, i)),
        out_shape=jax.ShapeDtypeStruct((picks, n), F32),
        scratch_shapes=([pltpu.VMEM((picks, tg), F32)]
                        + [pltpu.VMEM((ROW_WORDS * picks, LANES), F32)] * PEER_U_GROUP),
        compiler_params=pltpu.CompilerParams(dimension_semantics=("parallel",),
                                             vmem_limit_bytes=VMEM_LIMIT),
        name="peer_u",
    )(eidx, tbl, h2r, gates)


def _sc_gather_rows(table, idx):
    b, w = idx.shape[0], table.shape[1]
    workers = SC_CORES * SC_SUBCORES
    per_w = b // workers
    step = 2 * SC_ROWS
    assert b % workers == 0 and per_w % SC_IDX == 0 and SC_IDX % step == 0
    mesh = plsc.VectorSubcoreMesh(core_axis_name="c", subcore_axis_name="s")

    def body(table_hbm, idx_hbm, out_hbm, idx_v, rows0, rows1, g0, g1, w0, w1):
        wid = lax.axis_index("s") * SC_CORES + lax.axis_index("c")
        base = wid * per_w

        @pl.loop(0, per_w // SC_IDX)
        def _(o):
            off = pl.multiple_of(base + o * SC_IDX, SC_IDX)
            pltpu.sync_copy(idx_hbm.at[pl.ds(off, SC_IDX)], idx_v)

            @pl.loop(0, SC_IDX // step)
            def _(j):
                r0 = pl.multiple_of(j * step, step)
                r1 = pl.multiple_of(r0 + SC_ROWS, SC_ROWS)
                ga = pltpu.async_copy(table_hbm.at[idx_v.at[pl.ds(r0, SC_ROWS)]], rows0, g0)
                gb = pltpu.async_copy(table_hbm.at[idx_v.at[pl.ds(r1, SC_ROWS)]], rows1, g1)
                ga.wait()
                wa = pltpu.async_copy(rows0, out_hbm.at[pl.ds(pl.multiple_of(off + r0, SC_ROWS), SC_ROWS)], w0)
                gb.wait()
                wb = pltpu.async_copy(rows1, out_hbm.at[pl.ds(pl.multiple_of(off + r1, SC_ROWS), SC_ROWS)], w1)
                wa.wait()
                wb.wait()

    return pl.kernel(
        body,
        out_type=jax.ShapeDtypeStruct((b, w), table.dtype),
        mesh=mesh,
        scratch_types=[pltpu.VMEM((SC_IDX,), jnp.int32), pltpu.VMEM((SC_ROWS, w), table.dtype),
                       pltpu.VMEM((SC_ROWS, w), table.dtype)] + [pltpu.SemaphoreType.DMA] * 4,
        name="sc_gather_rows",
    )(table, idx)


def _peer_v_dense_kernel(rows_ref, coef_ref, x1_ref, o_ref, *, tt, picks):
    parts = _dense_parts(coef_ref[...])
    for t in range(tt):
        o_ref[t:t + 1, :] = _dense_token(rows_ref[t * picks:(t + 1) * picks, :],
                                         [part[t:t + 1, :] for part in parts], x1_ref[t:t + 1, :])


def _peer_v_dense(rows, coef_t, x1, tt):
    n, picks = coef_t.shape
    d = x1.shape[1]
    kern = functools.partial(_peer_v_dense_kernel, tt=tt, picks=picks)
    return pl.pallas_call(
        kern,
        grid=(n // tt,),
        in_specs=[pl.BlockSpec((tt * picks, d // 2), lambda i: (i, 0)),
                  pl.BlockSpec((tt, picks), lambda i: (i, 0)),
                  pl.BlockSpec((tt, d), lambda i: (i, 0))],
        out_specs=pl.BlockSpec((tt, d), lambda i: (i, 0)),
        out_shape=jax.ShapeDtypeStruct((n, d), F32),
        compiler_params=pltpu.CompilerParams(dimension_semantics=("parallel",),
                                             vmem_limit_bytes=VMEM_LIMIT),
        name="peer_v_dense",
    )(rows, coef_t, x1)


def _pack_table(t):
    e, d = t.shape
    word = lambda x: lax.bitcast_convert_type(x.astype(BF16), jnp.uint16).astype(jnp.uint32)
    packed = word(t[:, :d // 2]) | (word(t[:, d // 2:]) << 16)
    return packed.reshape(e, d // (2 * LANES), LANES)


def _tile(n, pref):
    t = min(n, pref)
    assert n % t == 0, (n, pref)
    return t


def kernel(x, mem, attn_norm_g, w_in, conv_w, conv_b, conv_ln_g, conv_ln_b, sb_out_g, mem_norm_g, w_mem_kv, q_norm_g, k_norm_g, mem_out_g, w_out, ffn_norm_g, w_peer_q, peer_keys1, peer_keys2, peer_u, peer_v):
    batch, seq, d = x.shape
    depth = w_in.shape[0]
    conv_ch = conv_w.shape[2]
    sb_width = sb_out_g.shape[1]
    mem_width = mem_out_g.shape[1]
    assert d == SUBLANES * LANES and PEER_TOPK == 2 * SUBLANES and conv_ch % LANES == 0 and sb_width % LANES == 0
    n = batch * seq
    q_off, k_off, v_off = 2 * conv_ch, 2 * conv_ch + sb_width, 2 * conv_ch + 2 * sb_width
    m_off = 2 * conv_ch + 3 * sb_width
    row = lambda v: v.reshape(1, -1)

    x2 = x.reshape(n, d)
    for l in range(depth):
        proj = _in_proj(x2, row(attn_norm_g[l]), w_in[l].astype(BF16), _tile(n, 512))
        conv_o = _conv(proj, conv_w[l], row(conv_b[l]), row(conv_ln_g[l]), row(conv_ln_b[l]),
                       batch, seq, _tile(seq, 512))
        sb = _stickbreak(proj, batch, seq, _tile(seq, 256), q_off // LANES, k_off // LANES,
                         v_off // LANES, sb_width // LANES)
        heads_m = mem_width // HEAD_DIM
        mk, mv = _mem_kv(mem, row(mem_norm_g[l]), w_mem_kv[l].astype(BF16),
                         row(jnp.tile(k_norm_g[l], heads_m)))
        mem_o = _mem_attn(proj, mk, mv, row(jnp.tile(q_norm_g[l], heads_m)), row(mem_out_g[l]),
                          batch, seq, _tile(seq, 512), m_off // mem_width)
        x1, h2, q = _out_proj(conv_o, sb, mem_o, x2, row(sb_out_g[l]), w_out[l].astype(BF16),
                              row(ffn_norm_g[l]), w_peer_q[l].astype(BF16), _tile(n, 512))
        eidx, gates = _peer_topk(q, peer_keys1[l].astype(BF16), peer_keys2[l].astype(BF16),
                                 _tile(n, 1024))
        tg = _tile(n, 512)
        eidx_t = eidx.T
        v_rows = _sc_gather_rows(_pack_table(peer_v[l]).reshape(peer_v.shape[1], d // 2),
                                 eidx_t.reshape(-1))
        coef = _peer_u(eidx_t, _pack_table(peer_u[l]), h2.reshape(n, SUBLANES, LANES), gates, tg)
        x2 = _peer_v_dense(v_rows, coef.T, x1, _tile(n, 32))
    return x2.reshape(batch, seq, d)
```

```python
import functools

import jax
import jax.numpy as jnp
from jax import lax
from jax.experimental import pallas as pl
from jax.experimental.pallas import tpu as pltpu
from jax.experimental.pallas import tpu_sc as plsc

F32 = jnp.float32
BF16 = jnp.bfloat16
EPS = 1e-6

HEAD_DIM = 64
LANES = 128
SUBLANES = 8
BF16_ROWS = 16
ROW_WORDS = 4
CONV_HALO = 32
SB_ROWS = 128
SB_DEAD = -120.0
SC_CORES, SC_SUBCORES = 2, 16
SC_ROWS = 64
SC_IDX = 4096
PEER_U_GROUP = 8
PEER_TOPK = 16
N_KEYS = 128
NEG_INF = float("-inf")
BIG_ID = float(2 ** 24)
VMEM_LIMIT = 56 * 1024 * 1024


def _rms(x, g):
    return x * lax.rsqrt(jnp.mean(x * x, axis=-1, keepdims=True) + EPS) * g


def _dot_nt(a, b):
    return lax.dot_general(a, b, (((1,), (1,)), ((), ())), preferred_element_type=F32)


def _split_bf16(x):
    hi = x.astype(BF16)
    lo = (x - hi.astype(F32)).astype(BF16)
    return hi, lo


def _in_proj_kernel(x_ref, g_ref, w_ref, o_ref):
    h = _rms(x_ref[...], g_ref[...])
    o_ref[...] = jnp.dot(h.astype(BF16), w_ref[...], preferred_element_type=F32).astype(o_ref.dtype)


def _in_proj(x2, g, w, tm):
    n, d = x2.shape
    width = w.shape[1]
    return pl.pallas_call(
        _in_proj_kernel,
        grid=(n // tm,),
        in_specs=[pl.BlockSpec((tm, d), lambda i: (i, 0)),
                  pl.BlockSpec((1, d), lambda i: (0, 0)),
                  pl.BlockSpec((d, width), lambda i: (0, 0))],
        out_specs=pl.BlockSpec((tm, width), lambda i: (i, 0)),
        out_shape=jax.ShapeDtypeStruct((n, width), BF16),
        compiler_params=pltpu.CompilerParams(dimension_semantics=("parallel",),
                                             vmem_limit_bytes=VMEM_LIMIT),
        name="in_proj",
    )(x2, g, w)


def _conv_kernel(a_ref, gate_ref, w_ref, b_ref, lg_ref, lb_ref, o_ref, ubuf, *, tc, width):
    i = pl.program_id(1)

    @pl.when(i == 0)
    def _():
        ubuf[0:CONV_HALO, :] = jnp.zeros((CONV_HALO, ubuf.shape[1]), F32)

    @pl.when(i > 0)
    def _():
        ubuf[0:CONV_HALO, :] = ubuf[tc:tc + CONV_HALO, :]

    a = a_ref[...].astype(F32)
    gate = gate_ref[...].astype(F32)
    ubuf[CONV_HALO:CONV_HALO + tc, :] = a * jax.nn.sigmoid(gate)
    base = CONV_HALO - (width - 1)
    acc = jnp.zeros((tc, ubuf.shape[1]), F32)
    for s in range(SUBLANES):
        taps = [j for j in range(width) if (base + j) % SUBLANES == s]
        if not taps:
            continue
        first = base + taps[0]
        window = ubuf[first:base + taps[-1] + tc, :]
        for j in taps:
            off = base + j - first
            acc = acc + w_ref[j:j + 1, :] * window[off:off + tc, :]
    y = acc + b_ref[...]
    mu = jnp.mean(y, axis=-1, keepdims=True)
    yc = y - mu
    var = jnp.mean(yc * yc, axis=-1, keepdims=True)
    z = yc * lax.rsqrt(var + EPS) * lg_ref[...] + lb_ref[...]
    o_ref[...] = (z * jax.nn.sigmoid(z)).astype(o_ref.dtype)


def _conv(proj, conv_w, conv_b, ln_g, ln_b, batch, seq, tc):
    n = proj.shape[0]
    width, ch = conv_w.shape
    nt = seq // tc
    kern = functools.partial(_conv_kernel, tc=tc, width=width)
    vec = lambda: pl.BlockSpec((1, ch), lambda b, i: (0, 0))
    return pl.pallas_call(
        kern,
        grid=(batch, nt),
        in_specs=[pl.BlockSpec((tc, ch), lambda b, i: (b * nt + i, 0)),
                  pl.BlockSpec((tc, ch), lambda b, i: (b * nt + i, 1)),
                  pl.BlockSpec((width, ch), lambda b, i: (0, 0)),
                  vec(), vec(), vec()],
        out_specs=pl.BlockSpec((tc, ch), lambda b, i: (b * nt + i, 0)),
        out_shape=jax.ShapeDtypeStruct((n, ch), BF16),
        scratch_shapes=[pltpu.VMEM((tc + CONV_HALO, ch), F32)],
        compiler_params=pltpu.CompilerParams(dimension_semantics=("parallel", "arbitrary"),
                                             vmem_limit_bytes=VMEM_LIMIT),
        name="conv",
    )(proj, proj, conv_w, conv_b, ln_g, ln_b)


def _sb_kernel(q_ref, k_ref, v_ref, o_ref, *scratch, tq):
    n_chain = len(scratch) // 2
    acc_refs, carry_refs = scratch[:n_chain], scratch[n_chain:]
    qi = pl.program_id(2)
    q2 = q_ref[...]
    lane = lax.broadcasted_iota(jnp.int32, (1, LANES), 1)
    row = lax.broadcasted_iota(jnp.int32, (tq, tq), 0)
    col = lax.broadcasted_iota(jnp.int32, (tq, tq), 1)
    suffix = jnp.where(row > col, -1.0, 0.0).astype(BF16)
    scale = jnp.asarray(HEAD_DIM ** -0.5, q2.dtype)
    chains = []
    for h in range(2):
        head = (lane >= h * HEAD_DIM) & (lane < (h + 1) * HEAD_DIM)
        qh = jnp.where(head, q2, jnp.zeros_like(q2)) * scale
        for r0 in range(0, tq, SB_ROWS):
            chains.append((qh[r0:r0 + SB_ROWS], r0))
    for c in range(len(chains)):
        acc_refs[c][...] = jnp.zeros(acc_refs[c].shape, F32)
        carry_refs[c][...] = jnp.zeros(carry_refs[c].shape, F32)

    def stages(kb, diag):
        st = {}
        n = len(chains)

        def qk():
            start = pl.multiple_of(kb * tq, tq)
            st["k"] = k_ref[pl.ds(start, tq), :]
            st["v"] = v_ref[pl.ds(start, tq), :]
            st["z"] = [_dot_nt(qc, st["k"]) for qc, _ in chains]

        def logs():
            st["lg"], st["spb"], st["keep"] = [], [], []
            for z, (_, r0) in zip(st["z"], chains):
                lg = jnp.minimum(z, 0.0) - jnp.log(1.0 + jnp.exp(jnp.minimum(z, -z)))
                sp = z - lg
                if diag:
                    keep = (lax.broadcasted_iota(jnp.int32, (SB_ROWS, tq), 1)
                            < lax.broadcasted_iota(jnp.int32, (SB_ROWS, tq), 0) + r0)
                    sp = jnp.where(keep, sp, 0.0)
                    st["keep"].append(keep)
                st["lg"].append(lg)
                st["spb"].append(sp.astype(BF16))

        def sums():
            st["r"] = [jnp.dot(spb, suffix, preferred_element_type=F32) for spb in st["spb"]]

        def weights():
            st["a"] = []
            for c in range(n):
                after = st["r"][c] + carry_refs[c][...]
                a = jnp.exp(st["lg"][c] + after)
                if diag:
                    a = jnp.where(st["keep"][c], a, 0.0)
                st["a"].append(a.astype(BF16))
                carry_refs[c][...] = after[:, 0:1] - st["spb"][c][:, 0:LANES].astype(F32)[:, 0:1]

        def values():
            for c in range(n):
                acc_refs[c][...] += jnp.dot(st["a"][c], st["v"], preferred_element_type=F32)

        return [qk, logs, sums, weights, values]

    def run(blocks):
        seqs = [stages(kb, diag) for kb, diag in blocks]
        depth = len(seqs[0])
        for step in range(depth + len(seqs) - 1):
            for i, sq in enumerate(seqs):
                if 0 <= step - i < depth:
                    sq[step - i]()

    run([(qi, True)])

    def alive():
        top = functools.reduce(jnp.maximum, [jnp.max(r[...]) for r in carry_refs])
        return top > SB_DEAD

    @pl.when(qi >= 1)
    def _():
        run([(qi - 1, False)])

    rest = jnp.maximum(qi - 1, 0)

    def live(s):
        i, go = s
        return (i < rest // 2) & go

    def pair(s):
        i, _ = s
        run([(qi - 2 - 2 * i, False), (qi - 3 - 2 * i, False)])
        return i + 1, alive()

    _, go = lax.while_loop(live, pair, (jnp.int32(0), alive()))

    @pl.when((rest % 2 == 1) & go)
    def _():
        run([(0, False)])

    n_sub = tq // SB_ROWS
    out = [jnp.concatenate([acc_refs[h * n_sub + s][...] for s in range(n_sub)], axis=0) for h in range(2)]
    o_ref[...] = jnp.where(lane < HEAD_DIM, out[0], out[1]).astype(o_ref.dtype)


def _stickbreak(proj, batch, seq, tq, q_col, k_col, v_col, n_pairs):
    n = proj.shape[0]
    nq = seq // tq
    kern = functools.partial(_sb_kernel, tq=tq)
    return pl.pallas_call(
        kern,
        grid=(batch, n_pairs, nq),
        in_specs=[pl.BlockSpec((tq, LANES), lambda b, p, i: (b * nq + i, q_col + p)),
                  pl.BlockSpec((seq, LANES), lambda b, p, i: (b, k_col + p)),
                  pl.BlockSpec((seq, LANES), lambda b, p, i: (b, v_col + p))],
        out_specs=pl.BlockSpec((tq, LANES), lambda b, p, i: (b * nq + i, p)),
        out_shape=jax.ShapeDtypeStruct((n, n_pairs * LANES), F32),
        scratch_shapes=([pltpu.VMEM((SB_ROWS, LANES), F32)] * (2 * tq // SB_ROWS)
                        + [pltpu.VMEM((SB_ROWS, 1), F32)] * (2 * tq // SB_ROWS)),
        compiler_params=pltpu.CompilerParams(
            dimension_semantics=("parallel", "parallel", "arbitrary"),
            vmem_limit_bytes=VMEM_LIMIT),
        name="stickbreak",
    )(proj, proj, proj)


def _head_sumsq(x, blk_ones):
    hi, lo = _split_bf16(x * x)
    return (jnp.dot(hi, blk_ones, preferred_element_type=F32)
            + jnp.dot(lo, blk_ones, preferred_element_type=F32))


def _block_ones(width):
    r = lax.broadcasted_iota(jnp.int32, (width, width), 0) // HEAD_DIM
    c = lax.broadcasted_iota(jnp.int32, (width, width), 1) // HEAD_DIM
    return jnp.where(r == c, 1.0, 0.0).astype(BF16)


def _mem_kv_kernel(mem_ref, g_ref, w_ref, kg_ref, k_ref, v_ref, *, width):
    mn = _rms(mem_ref[...], g_ref[...])
    kv = jnp.dot(mn.astype(BF16), w_ref[...], preferred_element_type=F32)
    k = kv[:, :width]
    ss = _head_sumsq(k, _block_ones(width))
    k_ref[...] = (k * lax.rsqrt(ss * (1.0 / HEAD_DIM) + EPS) * kg_ref[...]).astype(k_ref.dtype)
    v_ref[...] = kv[:, width:].astype(v_ref.dtype)


def _mem_kv(mem, g, w, kg_t):
    b, m, d = mem.shape
    width = w.shape[1] // 2
    kern = functools.partial(_mem_kv_kernel, width=width)
    out = jax.ShapeDtypeStruct((b, m, width), BF16)
    return pl.pallas_call(
        kern,
        grid=(b,),
        in_specs=[pl.BlockSpec((None, m, d), lambda i: (i, 0, 0)),
                  pl.BlockSpec((1, d), lambda i: (0, 0)),
                  pl.BlockSpec((d, 2 * width), lambda i: (0, 0)),
                  pl.BlockSpec((1, width), lambda i: (0, 0))],
        out_specs=[pl.BlockSpec((None, m, width), lambda i: (i, 0, 0)),
                   pl.BlockSpec((None, m, width), lambda i: (i, 0, 0))],
        out_shape=[out, out],
        compiler_params=pltpu.CompilerParams(dimension_semantics=("parallel",),
                                             vmem_limit_bytes=VMEM_LIMIT),
        name="mem_kv",
    )(mem, g, w, kg_t)


def _mem_attn_kernel(q_ref, k_ref, v_ref, qg_ref, og_ref, o_ref, *, width):
    q = q_ref[...].astype(F32)
    ss = _head_sumsq(q, _block_ones(width))
    qn = q * lax.rsqrt(ss * (1.0 / HEAD_DIM) + EPS) * qg_ref[...]
    k = k_ref[...]
    v = v_ref[...]
    lane = lax.broadcasted_iota(jnp.int32, (1, width), 1)
    out = jnp.zeros(q.shape, F32)
    for h in range(width // HEAD_DIM):
        head = (lane >= h * HEAD_DIM) & (lane < (h + 1) * HEAD_DIM)
        qh = jnp.where(head, qn, 0.0).astype(BF16)
        s = _dot_nt(qh, k) * (HEAD_DIM ** -0.5)
        p = jnp.exp(s - jnp.max(s, axis=-1, keepdims=True))
        o = jnp.dot(p.astype(BF16), v, preferred_element_type=F32) / jnp.sum(p, axis=-1, keepdims=True)
        out = jnp.where(head, o, out)
    o_ref[...] = _rms(out, og_ref[...]).astype(o_ref.dtype)


def _mem_attn(proj, k, v, qg_t, og, batch, seq, tm, q_col):
    n = proj.shape[0]
    _, m, width = k.shape
    nt = seq // tm
    kern = functools.partial(_mem_attn_kernel, width=width)
    return pl.pallas_call(
        kern,
        grid=(batch, nt),
        in_specs=[pl.BlockSpec((tm, width), lambda b, i: (b * nt + i, q_col)),
                  pl.BlockSpec((None, m, width), lambda b, i: (b, 0, 0)),
                  pl.BlockSpec((None, m, width), lambda b, i: (b, 0, 0)),
                  pl.BlockSpec((1, width), lambda b, i: (0, 0)),
                  pl.BlockSpec((1, width), lambda b, i: (0, 0))],
        out_specs=pl.BlockSpec((tm, width), lambda b, i: (b * nt + i, 0)),
        out_shape=jax.ShapeDtypeStruct((n, width), BF16),
        compiler_params=pltpu.CompilerParams(dimension_semantics=("parallel", "parallel"),
                                             vmem_limit_bytes=VMEM_LIMIT),
        name="mem_attn",
    )(proj, k, v, qg_t, og)


def _out_proj_kernel(conv_ref, sb_ref, mem_ref, x_ref, sbg_ref, wo_ref, fg_ref, wq_ref,
                     x1_ref, h2_ref, q_ref, *, c_w, s_w):
    sbn = _rms(sb_ref[...], sbg_ref[...]).astype(BF16)
    mixed = jnp.dot(conv_ref[...], wo_ref[0:c_w, :], preferred_element_type=F32)
    mixed = mixed + jnp.dot(sbn, wo_ref[c_w:c_w + s_w, :], preferred_element_type=F32)
    mixed = mixed + jnp.dot(mem_ref[...], wo_ref[c_w + s_w:, :], preferred_element_type=F32)
    x1 = x_ref[...] + mixed
    h2 = _rms(x1, fg_ref[...])
    x1_ref[...] = x1
    h2_ref[...] = h2
    q_ref[...] = jnp.dot(h2.astype(BF16), wq_ref[...], preferred_element_type=F32).astype(q_ref.dtype)


def _out_proj(conv_o, sb, mem_o, x2, sbg, w_out, fg, w_q, tm):
    n, d = x2.shape
    c_w, s_w, m_w = conv_o.shape[1], sb.shape[1], mem_o.shape[1]
    qw = w_q.shape[1]
    kern = functools.partial(_out_proj_kernel, c_w=c_w, s_w=s_w)
    row = lambda w: pl.BlockSpec((tm, w), lambda i: (i, 0))
    full = lambda r, c: pl.BlockSpec((r, c), lambda i: (0, 0))
    return pl.pallas_call(
        kern,
        grid=(n // tm,),
        in_specs=[row(c_w), row(s_w), row(m_w), row(d), full(1, s_w), full(c_w + s_w + m_w, d),
                  full(1, d), full(d, qw)],
        out_specs=[row(d), row(d), row(qw)],
        out_shape=[jax.ShapeDtypeStruct((n, d), F32), jax.ShapeDtypeStruct((n, d), F32),
                   jax.ShapeDtypeStruct((n, qw), BF16)],
        compiler_params=pltpu.CompilerParams(dimension_semantics=("parallel",),
                                             vmem_limit_bytes=VMEM_LIMIT),
        name="out_proj",
    )(conv_o, sb, mem_o, x2, sbg, w_out, fg, w_q)


def _topk_rows(s, ids, k):
    vals, sel = [], []
    for _ in range(k):
        m = jnp.max(s, axis=0, keepdims=True)
        i = jnp.min(jnp.where(s == m, ids, BIG_ID), axis=0, keepdims=True)
        s = jnp.where(ids == i, NEG_INF, s)
        vals.append(m)
        sel.append(i)
    return jnp.concatenate(vals, axis=0), jnp.concatenate(sel, axis=0)


def _peer_topk_kernel(q_ref, k1_ref, k2_ref, eidx_ref, gates_ref, *, half):
    t = q_ref.shape[0]
    key_ids = lax.broadcasted_iota(jnp.int32, (N_KEYS, t), 0).astype(F32)
    q = q_ref[...]
    v1, i1 = _topk_rows(_dot_nt(k1_ref[...], q[:, :half]), key_ids, PEER_TOPK)
    v2, i2 = _topk_rows(_dot_nt(k2_ref[...], q[:, half:]), key_ids, PEER_TOPK)
    cand_s, cand_i = [v1[0:1] + v2], [i1[0:1] * N_KEYS + i2]
    sub = lax.broadcasted_iota(jnp.int32, (SUBLANES, t), 0)
    for a in range(1, SUBLANES):
        ok = sub < (PEER_TOPK // (a + 1))
        cand_s.append(jnp.where(ok, v1[a:a + 1] + v2[0:SUBLANES], NEG_INF))
        cand_i.append(i1[a:a + 1] * N_KEYS + i2[0:SUBLANES])
    cand_s.append(v1[SUBLANES:] + v2[0:1])
    cand_i.append(i1[SUBLANES:] * N_KEYS + i2[0:1])
    top_s, eidx = _topk_rows(jnp.concatenate(cand_s, axis=0), jnp.concatenate(cand_i, axis=0), PEER_TOPK)
    p = jnp.exp(top_s - top_s[0:1])
    eidx_ref[...] = eidx.astype(jnp.int32)
    gates_ref[...] = p / jnp.sum(p, axis=0, keepdims=True)


def _peer_topk(q, k1, k2, tt):
    n, qw = q.shape
    heads, n_keys, half = k1.shape
    kern = functools.partial(_peer_topk_kernel, half=half)
    out_spec = lambda: pl.BlockSpec((PEER_TOPK, tt), lambda i, h: (h, i))
    return pl.pallas_call(
        kern,
        grid=(n // tt, heads),
        in_specs=[pl.BlockSpec((tt, 2 * half), lambda i, h: (i, h)),
                  pl.BlockSpec((None, n_keys, half), lambda i, h: (h, 0, 0)),
                  pl.BlockSpec((None, n_keys, half), lambda i, h: (h, 0, 0))],
        out_specs=[out_spec(), out_spec()],
        out_shape=[jax.ShapeDtypeStruct((heads * PEER_TOPK, n), jnp.int32),
                   jax.ShapeDtypeStruct((heads * PEER_TOPK, n), F32)],
        compiler_params=pltpu.CompilerParams(dimension_semantics=("parallel", "parallel"),
                                             vmem_limit_bytes=VMEM_LIMIT),
        name="peer_topk",
    )(q, k1, k2)


def _unpack_pair(w):
    lo = pltpu.bitcast(w << 16, F32)
    hi = pltpu.bitcast(w & jnp.uint32(0xFFFF0000), F32)
    return lo, hi


def _spread_matrices(picks):
    pr = lax.broadcasted_iota(jnp.int32, (picks, 2 * picks), 0)
    pc = lax.broadcasted_iota(jnp.int32, (picks, 2 * picks), 1)
    return (jnp.where(pc == 2 * pr, 1.0, 0.0).astype(BF16),
            jnp.where(pc == 2 * pr + 1, 1.0, 0.0).astype(BF16))


def _dense_parts(coef):
    hi, lo = _split_bf16(coef)
    return [jnp.dot(x, s, preferred_element_type=F32)
            for s in _spread_matrices(coef.shape[1]) for x in (hi, lo)]


def _dense_token(rows, part_rows, x1_row):
    b = pltpu.bitcast(rows, BF16)
    sub = lax.broadcasted_iota(jnp.int32, (BF16_ROWS, b.shape[0]), 0)
    lhs = jnp.zeros((BF16_ROWS, b.shape[0]), F32)
    for k, row in enumerate(part_rows):
        lhs = jnp.where(sub == k, row, lhs)
    r = jnp.dot(lhs.astype(BF16), b, preferred_element_type=F32)
    return jnp.concatenate([r[0:1] + r[1:2], r[2:3] + r[3:4]], axis=1) + x1_row


def _peer_u_kernel(idx_ref, tbl_ref, h2_ref, gates_ref, coef_ref, act_ref, *y_refs, tg, picks):
    lane = lax.broadcasted_iota(jnp.int32, (1, LANES), 1)
    group = len(y_refs)

    def products(t, y_ref):
        hv = h2_ref[t]
        h_lo, h_hi = hv[0:ROW_WORDS], hv[ROW_WORDS:]
        for p in range(picks):
            lo, hi = _unpack_pair(tbl_ref[idx_ref[t, p]])
            y_ref[ROW_WORDS * p:ROW_WORDS * (p + 1), :] = lo * h_lo + hi * h_hi

    def reduce(t, y_ref):
        part = functools.reduce(jnp.add, [y_ref[pl.ds(r, picks, stride=ROW_WORDS), :]
                                          for r in range(ROW_WORDS)])
        act = jnp.sum(part, axis=1, keepdims=True)
        blk = pl.multiple_of(jnp.maximum(t, 0) // LANES * LANES, LANES)
        cols = pl.ds(blk, LANES)
        act_ref[:, cols] = jnp.where(lane == t - blk, act, act_ref[:, cols])

    y_refs[-1][...] = jnp.zeros(y_refs[-1].shape, F32)
    act_ref[...] = jnp.zeros(act_ref.shape, F32)

    def body(j, c):
        t0 = group * j
        for k in range(group):
            products(t0 + k, y_refs[k])
            reduce(t0 + k - 1, y_refs[k - 1])
        return c

    lax.fori_loop(0, tg // group, body, 0)
    reduce(tg - 1, y_refs[-1])
    act = act_ref[...]
    gelu = 0.5 * act * (1.0 + lax.erf(act * (2.0 ** -0.5)))
    coef_ref[...] = gates_ref[...] * gelu


def _peer_u(eidx, tbl, h2r, gates, tg):
    n, picks = eidx.shape
    kern = functools.partial(_peer_u_kernel, tg=tg, picks=picks)
    return pl.pallas_call(
        kern,
        grid=(n // tg,),
        in_specs=[pl.BlockSpec((tg, picks), lambda i: (i, 0), memory_space=pltpu.SMEM,
                               pipeline_mode=pl.Buffered(1)),
                  pl.BlockSpec(memory_space=pltpu.VMEM),
                  pl.BlockSpec((tg, SUBLANES, LANES), lambda i: (i, 0, 0)),
                  pl.BlockSpec((picks, tg), lambda i: (0, i))],
        out_specs=pl.BlockSpec((picks, tg), lambda i: (0, i)),
        out_shape=jax.ShapeDtypeStruct((picks, n), F32),
        scratch_shapes=([pltpu.VMEM((picks, tg), F32)]
                        + [pltpu.VMEM((ROW_WORDS * picks, LANES), F32)] * PEER_U_GROUP),
        compiler_params=pltpu.CompilerParams(dimension_semantics=("parallel",),
                                             vmem_limit_bytes=VMEM_LIMIT),
        name="peer_u",
    )(eidx, tbl, h2r, gates)


def _sc_gather_rows(table, idx):
    b, w = idx.shape[0], table.shape[1]
    workers = SC_CORES * SC_SUBCORES
    per_w = b // workers
    step = 2 * SC_ROWS
    assert b % workers == 0 and per_w % SC_IDX == 0 and SC_IDX % step == 0
    mesh = plsc.VectorSubcoreMesh(core_axis_name="c", subcore_axis_name="s")

    def body(table_hbm, idx_hbm, out_hbm, idx_v, rows0, rows1, g0, g1, w0, w1):
        wid = lax.axis_index("s") * SC_CORES + lax.axis_index("c")
        base = wid * per_w

        @pl.loop(0, per_w // SC_IDX)
        def _(o):
            off = pl.multiple_of(base + o * SC_IDX, SC_IDX)
            pltpu.sync_copy(idx_hbm.at[pl.ds(off, SC_IDX)], idx_v)

            @pl.loop(0, SC_IDX // step)
            def _(j):
                r0 = pl.multiple_of(j * step, step)
                r1 = pl.multiple_of(r0 + SC_ROWS, SC_ROWS)
                ga = pltpu.async_copy(table_hbm.at[idx_v.at[pl.ds(r0, SC_ROWS)]], rows0, g0)
                gb = pltpu.async_copy(table_hbm.at[idx_v.at[pl.ds(r1, SC_ROWS)]], rows1, g1)
                ga.wait()
                wa = pltpu.async_copy(rows0, out_hbm.at[pl.ds(pl.multiple_of(off + r0, SC_ROWS), SC_ROWS)], w0)
                gb.wait()
                wb = pltpu.async_copy(rows1, out_hbm.at[pl.ds(pl.multiple_of(off + r1, SC_ROWS), SC_ROWS)], w1)
                wa.wait()
                wb.wait()

    return pl.kernel(
        body,
        out_type=jax.ShapeDtypeStruct((b, w), table.dtype),
        mesh=mesh,
        scratch_types=[pltpu.VMEM((SC_IDX,), jnp.int32), pltpu.VMEM((SC_ROWS, w), table.dtype),
                       pltpu.VMEM((SC_ROWS, w), table.dtype)] + [pltpu.SemaphoreType.DMA] * 4,
        name="sc_gather_rows",
    )(table, idx)


def _peer_v_dense_kernel(rows_ref, coef_ref, x1_ref, o_ref, *, tt, picks):
    parts = _dense_parts(coef_ref[...])
    for t in range(tt):
        o_ref[t:t + 1, :] = _dense_token(rows_ref[t * picks:(t + 1) * picks, :],
                                         [part[t:t + 1, :] for part in parts], x1_ref[t:t + 1, :])


def _peer_v_dense(rows, coef_t, x1, tt):
    n, picks = coef_t.shape
    d = x1.shape[1]
    kern = functools.partial(_peer_v_dense_kernel, tt=tt, picks=picks)
    return pl.pallas_call(
        kern,
        grid=(n // tt,),
        in_specs=[pl.BlockSpec((tt * picks, d // 2), lambda i: (i, 0)),
                  pl.BlockSpec((tt, picks), lambda i: (i, 0)),
                  pl.BlockSpec((tt, d), lambda i: (i, 0))],
        out_specs=pl.BlockSpec((tt, d), lambda i: (i, 0)),
        out_shape=jax.ShapeDtypeStruct((n, d), F32),
        compiler_params=pltpu.CompilerParams(dimension_semantics=("parallel",),
                                             vmem_limit_bytes=VMEM_LIMIT),
        name="peer_v_dense",
    )(rows, coef_t, x1)


def _pack_table(t):
    e, d = t.shape
    word = lambda x: lax.bitcast_convert_type(x.astype(BF16), jnp.uint16).astype(jnp.uint32)
    packed = word(t[:, :d // 2]) | (word(t[:, d // 2:]) << 16)
    return packed.reshape(e, d // (2 * LANES), LANES)


def _tile(n, pref):
    t = min(n, pref)
    assert n % t == 0, (n, pref)
    return t


def kernel(x, mem, attn_norm_g, w_in, conv_w, conv_b, conv_ln_g, conv_ln_b, sb_out_g, mem_norm_g, w_mem_kv, q_norm_g, k_norm_g, mem_out_g, w_out, ffn_norm_g, w_peer_q, peer_keys1, peer_keys2, peer_u, peer_v):
    batch, seq, d = x.shape
    depth = w_in.shape[0]
    conv_ch = conv_w.shape[2]
    sb_width = sb_out_g.shape[1]
    mem_width = mem_out_g.shape[1]
    assert d == SUBLANES * LANES and PEER_TOPK == 2 * SUBLANES and conv_ch % LANES == 0 and sb_width % LANES == 0
    n = batch * seq
    q_off, k_off, v_off = 2 * conv_ch, 2 * conv_ch + sb_width, 2 * conv_ch + 2 * sb_width
    m_off = 2 * conv_ch + 3 * sb_width
    row = lambda v: v.reshape(1, -1)

    x2 = x.reshape(n, d)
    for l in range(depth):
        proj = _in_proj(x2, row(attn_norm_g[l]), w_in[l].astype(BF16), _tile(n, 512))
        conv_o = _conv(proj, conv_w[l], row(conv_b[l]), row(conv_ln_g[l]), row(conv_ln_b[l]),
                       batch, seq, _tile(seq, 512))
        sb = _stickbreak(proj, batch, seq, _tile(seq, 256), q_off // LANES, k_off // LANES,
                         v_off // LANES, sb_width // LANES)
        heads_m = mem_width // HEAD_DIM
        mk, mv = _mem_kv(mem, row(mem_norm_g[l]), w_mem_kv[l].astype(BF16),
                         row(jnp.tile(k_norm_g[l], heads_m)))
        mem_o = _mem_attn(proj, mk, mv, row(jnp.tile(q_norm_g[l], heads_m)), row(mem_out_g[l]),
                          batch, seq, _tile(seq, 512), m_off // mem_width)
        x1, h2, q = _out_proj(conv_o, sb, mem_o, x2, row(sb_out_g[l]), w_out[l].astype(BF16),
                              row(ffn_norm_g[l]), w_peer_q[l].astype(BF16), _tile(n, 512))
        eidx, gates = _peer_topk(q, peer_keys1[l].astype(BF16), peer_keys2[l].astype(BF16),
                                 _tile(n, 1024))
        tg = _tile(n, 1024)
        eidx_t = eidx.T
        v_rows = _sc_gather_rows(_pack_table(peer_v[l]).reshape(peer_v.shape[1], d // 2),
                                 eidx_t.reshape(-1))
        coef = _peer_u(eidx_t, _pack_table(peer_u[l]), h2.reshape(n, SUBLANES, LANES), gates, tg)
        x2 = _peer_v_dense(v_rows, coef.T, x1, _tile(n, 32))
    return x2.reshape(batch, seq, d)
```
